```python
import jax, jax.numpy as jnp
from jax import lax
import numpy as np

D_MODEL = 1024
BATCH = 8
SEQ = 4096
DEPTH = 1

N_META = 16
D_MIX = D_MODEL
D_RG = D_MIX // 2
RG_HEADS = 8
RG_HEAD_DIM = D_RG // RG_HEADS
CONV_W = 4
LRU_C = 8.0
D_HG = D_MIX - D_RG
HG_HEAD_DIM = 128
HG_HEADS = D_HG // HG_HEAD_DIM
HG_CHUNK = 64
D_IN = 2 * D_RG + 4 * D_HG
D_FF = ((8 * D_MODEL // 3 + 255) // 256) * 256
EPS = 1e-6

kernel_name = "hymba_rglru_hgrn2_block"


def rmsnorm(x, g):
    xf = x.astype(jnp.float32)
    y = xf * lax.rsqrt(jnp.mean(xf * xf, axis=-1, keepdims=True) + EPS) * g.astype(jnp.float32)
    return y.astype(x.dtype)


def _lin_combine(e1, e2):
    a1, b1 = e1
    a2, b2 = e2
    return a1 * a2, a2 * b1 + b2


def rg_lru_group(xr, gr, conv_w, conv_b, w_r, b_r, w_i, b_i, lam, norm_g):
    B, L, _ = xr.shape
    xp = jnp.pad(xr.astype(jnp.float32), ((0, 0), (CONV_W - 1, 0), (0, 0)))
    cw = conv_w.astype(jnp.float32)
    xc = conv_b.astype(jnp.float32) + sum(xp[:, j:j + L] * cw[j] for j in range(CONV_W))
    xh = xc.reshape(B, L, RG_HEADS, RG_HEAD_DIM)
    r = jax.nn.sigmoid(jnp.einsum('blhi,hij->blhj', xh, w_r.astype(jnp.float32)).reshape(B, L, D_RG) + b_r.astype(jnp.float32))
    i = jax.nn.sigmoid(jnp.einsum('blhi,hij->blhj', xh, w_i.astype(jnp.float32)).reshape(B, L, D_RG) + b_i.astype(jnp.float32))
    log_a = -LRU_C * jax.nn.softplus(-lam.astype(jnp.float32)) * r
    a = jnp.exp(log_a)
    bx = jnp.sqrt(-jnp.expm1(2.0 * log_a)) * (i * xc)
    _, h = lax.associative_scan(_lin_combine, (a, bx), axis=1)
    y = jax.nn.gelu(gr.astype(jnp.float32)) * h
    return rmsnorm(y, norm_g)


def _to_chunks(t, pad):
    B, L, _ = t.shape
    t = jnp.pad(t, ((0, 0), (pad, 0), (0, 0)))
    n = (L + pad) // HG_CHUNK
    t = t.reshape(B, n, HG_CHUNK, HG_HEADS, HG_HEAD_DIM)
    return jnp.transpose(t, (1, 0, 3, 2, 4))


def _hgrn2_chunk_step(S, inp):
    q, k, v, lf = inp
    b = jnp.cumsum(lf, axis=2)
    inter = jnp.einsum('bhck,bhkv->bhcv', q * jnp.exp(b), S)
    diff = b[:, :, :, None, :] - b[:, :, None, :, :]
    causal = (jnp.arange(HG_CHUNK)[:, None] >= jnp.arange(HG_CHUNK)[None, :])[None, None, :, :, None]
    decay = jnp.where(causal, jnp.exp(jnp.where(causal, diff, 0.0)), 0.0)
    A = jnp.einsum('bhtsk,bhsk->bhts', q[:, :, :, None, :] * decay, k)
    intra = jnp.einsum('bhts,bhsv->bhtv', A, v)
    b_last = b[:, :, -1:, :]
    S_new = jnp.exp(b_last[:, :, 0, :])[..., None] * S + jnp.einsum('bhsk,bhsv->bhkv', k * jnp.exp(b_last - b), v)
    return S_new, inter + intra


def hgrn2_group(hq, hf, hi, hg, lb, norm_g):
    B, L, _ = hq.shape
    lb = lb.astype(jnp.float32)
    q = jax.nn.silu(hq.astype(jnp.float32))
    f = lb + (1.0 - lb) * jax.nn.sigmoid(hf.astype(jnp.float32))
    log_f = jnp.log(f)
    k = 1.0 - f
    v = hi.astype(jnp.float32)
    pad = HG_CHUNK - N_META
    qc, kc, vc, lfc = (_to_chunks(t, pad) for t in (q, k, v, log_f))
    S0 = jnp.zeros((B, HG_HEADS, HG_HEAD_DIM, HG_HEAD_DIM), jnp.float32)
    _, o = lax.scan(_hgrn2_chunk_step, S0, (qc, kc, vc, lfc))
    n = o.shape[0]
    o = jnp.transpose(o, (1, 0, 3, 2, 4)).reshape(B, n * HG_CHUNK, HG_HEADS, HG_HEAD_DIM)[:, pad:]
    o = rmsnorm(o, norm_g).astype(jnp.float32) * jax.nn.silu(hg.astype(jnp.float32).reshape(B, L, HG_HEADS, HG_HEAD_DIM))
    return o.reshape(B, L, D_HG)


def setup_inputs(seed: int = 0) -> dict:
    key = jax.random.key(seed)
    ks = jax.random.split(key, 24)
    f32 = jnp.float32
    nrm = lambda k, shape, s: s * jax.random.normal(k, shape, f32)
    u = jax.random.uniform(ks[9], (DEPTH, D_RG), f32, 0.9, 0.999)
    s = u ** (1.0 / LRU_C)
    lru_lambda = jnp.log(s) - jnp.log1p(-s)
    return {
        "x": jax.random.normal(ks[0], (BATCH, SEQ, D_MODEL), f32),
        "meta_tokens": nrm(ks[1], (N_META, D_MODEL), 1.0),
        "mix_norm_g": 1.0 + nrm(ks[2], (DEPTH, D_MODEL), 0.02),
        "w_in": nrm(ks[3], (DEPTH, D_MODEL, D_IN), D_MODEL ** -0.5),
        "conv_w": nrm(ks[4], (DEPTH, CONV_W, D_RG), CONV_W ** -0.5),
        "conv_b": nrm(ks[5], (DEPTH, D_RG), 0.01),
        "w_rgate": nrm(ks[6], (DEPTH, RG_HEADS, RG_HEAD_DIM, RG_HEAD_DIM), RG_HEAD_DIM ** -0.5),
        "b_rgate": nrm(ks[7], (DEPTH, D_RG), 0.01),
        "w_igate": nrm(ks[8], (DEPTH, RG_HEADS, RG_HEAD_DIM, RG_HEAD_DIM), RG_HEAD_DIM ** -0.5),
        "b_igate": nrm(ks[10], (DEPTH, D_RG), 0.01),
        "lru_lambda": lru_lambda,
        "rg_norm_g": 1.0 + nrm(ks[11], (DEPTH, D_RG), 0.02),
        "hg_lower_bound": nrm(ks[12], (DEPTH + 1, D_HG), 0.1),
        "hg_norm_g": 1.0 + nrm(ks[13], (DEPTH, HG_HEAD_DIM), 0.02),
        "w_out": nrm(ks[14], (DEPTH, D_MIX, D_MODEL), D_MIX ** -0.5),
        "ffn_norm_g": 1.0 + nrm(ks[15], (DEPTH, D_MODEL), 0.02),
        "w_gate_up": nrm(ks[16], (DEPTH, D_MODEL, 2 * D_FF), D_MODEL ** -0.5),
        "w_down": nrm(ks[17], (DEPTH, D_FF, D_MODEL), D_FF ** -0.5),
        "final_norm_g": 1.0 + nrm(ks[18], (D_MODEL,), 0.02),
    }


def reference(x, meta_tokens, mix_norm_g, w_in, conv_w, conv_b, w_rgate, b_rgate, w_igate, b_igate,
              lru_lambda, rg_norm_g, hg_lower_bound, hg_norm_g, w_out, ffn_norm_g, w_gate_up, w_down,
              final_norm_g):
    B = x.shape[0]
    meta = jnp.broadcast_to(meta_tokens.astype(x.dtype)[None], (B, N_META, D_MODEL))
    h = jnp.concatenate([meta, x], axis=1)
    lbs = jnp.cumsum(jax.nn.softmax(hg_lower_bound.astype(jnp.float32), axis=0), axis=0)
    splits = np.cumsum([D_RG, D_RG, D_HG, D_HG, D_HG])
    for l in range(DEPTH):
        u = rmsnorm(h, mix_norm_g[l])
        p = jnp.einsum('bld,de->ble', u, w_in[l])
        rg_x, rg_g, hq, hf, hi, hg = jnp.split(p, splits, axis=-1)
        y_rg = rg_lru_group(rg_x, rg_g, conv_w[l], conv_b[l], w_rgate[l], b_rgate[l],
                            w_igate[l], b_igate[l], lru_lambda[l], rg_norm_g[l])
        y_hg = hgrn2_group(hq, hf, hi, hg, lbs[l], hg_norm_g[l])
        y = jnp.concatenate([y_rg.astype(h.dtype), y_hg.astype(h.dtype)], axis=-1)
        h = h + jnp.einsum('ble,ed->bld', y, w_out[l])
        v = rmsnorm(h, ffn_norm_g[l])
        gate, up = jnp.split(jnp.einsum('bld,df->blf', v, w_gate_up[l]), 2, axis=-1)
        h = h + jnp.einsum('blf,fd->bld', jax.nn.silu(gate) * up, w_down[l])
    return rmsnorm(h, final_norm_g)[:, N_META:]
```

```python
import functools

import jax
import jax.numpy as jnp
from jax import lax
from jax.experimental import pallas as pl
from jax.experimental.pallas import tpu as pltpu

D_MODEL = 1024
N_META = 16
D_RG = 512
RG_HEADS = 8
RG_HEAD_DIM = 64
CONV_W = 4
LRU_C = 8.0
D_HG = 512
HG_HEAD_DIM = 128
HG_HEADS = 4
D_IN = 2 * D_RG + 4 * D_HG
D_FF = 2816
EPS = 1e-6

HG_CHUNK = 64
SUBLANES = 8
MIX_ROWS = 256
TOK_ROWS = 512
FF_TILE = 256
VMEM_LIMIT = 56 * 1024 * 1024

_F32 = jnp.float32
_BF16 = jnp.bfloat16
_NT = (((1,), (1,)), ((), ()))
_TN = (((0,), (0,)), ((), ()))


def _rms(x, g):
    return x * lax.rsqrt(jnp.mean(x * x, axis=-1, keepdims=True) + EPS) * g


def _sigmoid(x):
    return 1.0 / (1.0 + jnp.exp(-x))


def _silu(x):
    return x * _sigmoid(x)


def _gelu_tanh(x):
    c = 0.7978845608028654
    return 0.5 * x * (1.0 + jnp.tanh(c * (x + 0.044715 * (x * x * x))))


def _norm_proj_kernel(x_ref, g_ref, w_ref, o_ref):
    u = _rms(x_ref[...], g_ref[...]).astype(_BF16)
    o_ref[...] = jnp.dot(u, w_ref[...], preferred_element_type=_F32)


def _norm_proj(x2d, g, w_bf16, rows):
    t, d = x2d.shape
    n = w_bf16.shape[1]
    return pl.pallas_call(
        _norm_proj_kernel,
        grid=(t // rows,),
        in_specs=[
            pl.BlockSpec((rows, d), lambda i: (i, 0)),
            pl.BlockSpec((1, d), lambda i: (0, 0)),
            pl.BlockSpec((d, n), lambda i: (0, 0)),
        ],
        out_specs=pl.BlockSpec((rows, n), lambda i: (i, 0)),
        out_shape=jax.ShapeDtypeStruct((t, n), _F32),
        compiler_params=pltpu.CompilerParams(
            dimension_semantics=("arbitrary",), vmem_limit_bytes=VMEM_LIMIT),
        name="norm_proj",
    )(x2d, g, w_bf16)


def _rg_block(xr, gr, xbuf, acum, bcum, hs, h_ref, prm):
    conv_w, conv_b, w_gate, b_r, b_i, lam, norm_g = prm
    n = xr.shape[0]
    xbuf[SUBLANES:SUBLANES + n, :] = xr
    xc = conv_b + conv_w[3:4, :] * xr
    for j in range(1, CONV_W):
        xc = xc + conv_w[CONV_W - 1 - j:CONV_W - j, :] * xbuf[SUBLANES - j:SUBLANES - j + n, :]
    xbuf[0:SUBLANES, :] = xr[n - SUBLANES:n, :]

    gates = jnp.dot(xc.astype(_BF16), w_gate, preferred_element_type=_F32)
    r = _sigmoid(gates[:, :D_RG] + b_r)
    i = _sigmoid(gates[:, D_RG:] + b_i)
    z = -lam
    softplus = jnp.maximum(z, 0.0) + jnp.log1p(jnp.exp(-jnp.abs(z)))
    log_a = (-LRU_C * softplus) * r
    a = jnp.exp(log_a)
    bx = jnp.sqrt(-jnp.tanh(log_a) * (a * a + 1.0)) * (i * xc)

    pos = lax.broadcasted_iota(jnp.int32, (n, 1), 0) & (SUBLANES - 1)
    for s in (1, 2, 4):
        keep = pos >= s
        a_prev = pltpu.roll(a, s, 0)
        b_prev = pltpu.roll(bx, s, 0)
        bx = jnp.where(keep, a * b_prev + bx, bx)
        a = jnp.where(keep, a * a_prev, a)
    acum[0:n, :] = a
    bcum[0:n, :] = bx

    def group(g, h):
        rows = pl.ds(pl.multiple_of(g * SUBLANES, SUBLANES), SUBLANES)
        hg = acum[rows, :] * h + bcum[rows, :]
        hs[rows, :] = hg
        return jnp.broadcast_to(hg[SUBLANES - 1:SUBLANES, :], (SUBLANES, D_RG))

    h_ref[...] = lax.fori_loop(0, n // SUBLANES, group, h_ref[...])
    y = _gelu_tanh(gr) * hs[0:n, :]
    return _rms(y, norm_g)


def _hg_chunk(hq, hf, hi, lb, st_ref, head, valid=None):
    c = HG_CHUNK
    q = _silu(hq)
    f = lb + (1.0 - lb) * _sigmoid(hf)
    v = hi
    if valid is not None:
        q = jnp.where(valid, q, 0.0)
        f = jnp.where(valid, f, 1.0)
        v = jnp.where(valid, v, 0.0)
    k = 1.0 - f
    vb = v.astype(_BF16)

    row = lax.broadcasted_iota(jnp.int32, (c, 1), 0)
    ti = lax.broadcasted_iota(jnp.int32, (c, c), 0)
    si = lax.broadcasted_iota(jnp.int32, (c, c), 1)
    att = jnp.where(ti == si,
                    lax.dot_general(q.astype(_BF16), k.astype(_BF16), _NT, preferred_element_type=_F32),
                    0.0)
    tot = f
    pre = f
    suf = jnp.ones_like(f)
    m = 1
    while m < c:
        upper = (row & m) != 0
        lhs = jnp.where(upper, q * pre, 0.0).astype(_BF16)
        rhs = jnp.where(upper, 0.0, k * suf).astype(_BF16)
        prod = lax.dot_general(lhs, rhs, _NT, preferred_element_type=_F32)
        if 2 * m < c:
            shift = (2 * m).bit_length() - 1
            prod = jnp.where((ti >> shift) == (si >> shift), prod, 0.0)
        att = att + prod
        sib = jnp.where(upper, pltpu.roll(tot, m, 0), pltpu.roll(tot, c - m, 0))
        pre = jnp.where(upper, pre * sib, pre)
        suf = jnp.where(upper, suf, suf * sib)
        tot = tot * sib
        m *= 2

    st = st_ref[head]
    o = lax.dot_general((q * pre).astype(_BF16), st.astype(_BF16), _NT, preferred_element_type=_F32)
    o = o + jnp.dot(att.astype(_BF16), vb, preferred_element_type=_F32)
    upd = lax.dot_general(vb, (k * suf).astype(_BF16), _TN, preferred_element_type=_F32)
    st_ref[head] = st * tot[0:1, :] + upd
    return o


def _hg_lower_bound(lb2):
    m = jnp.maximum(lb2[0:1, :], lb2[1:2, :])
    e0 = jnp.exp(lb2[0:1, :] - m)
    e1 = jnp.exp(lb2[1:2, :] - m)
    return e0 / (e0 + e1)


def _meta_state_kernel(p_ref, cw_ref, cb_ref, wg_ref, br_ref, bi_ref, lam_ref, rgn_ref, lb_ref,
                       h_out, tail_out, st_out, xbuf, acum, bcum, hs):
    n = N_META
    xbuf[...] = jnp.zeros_like(xbuf)
    h_out[...] = jnp.zeros_like(h_out)
    prm = (cw_ref[...], cb_ref[...], wg_ref[...], br_ref[...], bi_ref[...], lam_ref[...], rgn_ref[...])
    _rg_block(p_ref[:, 0:D_RG], p_ref[:, D_RG:2 * D_RG], xbuf, acum, bcum, hs, h_out, prm)
    tail_out[...] = xbuf[0:SUBLANES, :]

    st_out[...] = jnp.zeros_like(st_out)
    lb = _hg_lower_bound(lb_ref[...])
    pad = HG_CHUNK - n
    valid = lax.broadcasted_iota(jnp.int32, (HG_CHUNK, 1), 0) >= pad
    zeros = jnp.zeros((pad, HG_HEAD_DIM), _F32)
    base = 2 * D_RG
    for hd in range(HG_HEADS):
        lo = hd * HG_HEAD_DIM
        cols = lambda part: slice(base + part * D_HG + lo, base + part * D_HG + lo + HG_HEAD_DIM)
        hq = jnp.concatenate([zeros, p_ref[:, cols(0)]], axis=0)
        hf = jnp.concatenate([zeros, p_ref[:, cols(1)]], axis=0)
        hi = jnp.concatenate([zeros, p_ref[:, cols(2)]], axis=0)
        _hg_chunk(hq, hf, hi, lb[:, lo:lo + HG_HEAD_DIM], st_out, hd, valid=valid)


def _meta_state(p_meta, cw, cb, wg, br, bi, lam, rgn, lb2):
    full = lambda a: pl.BlockSpec(a.shape, lambda: (0,) * a.ndim)
    args = (p_meta, cw, cb, wg, br, bi, lam, rgn, lb2)
    out_shapes = (
        jax.ShapeDtypeStruct((SUBLANES, D_RG), _F32),
        jax.ShapeDtypeStruct((SUBLANES, D_RG), _F32),
        jax.ShapeDtypeStruct((HG_HEADS, HG_HEAD_DIM, HG_HEAD_DIM), _F32),
    )
    return pl.pallas_call(
        _meta_state_kernel,
        in_specs=[full(a) for a in args],
        out_specs=tuple(pl.BlockSpec(s.shape, lambda n=len(s.shape): (0,) * n) for s in out_shapes),
        out_shape=out_shapes,
        scratch_shapes=[
            pltpu.VMEM((N_META + SUBLANES, D_RG), _F32),
            pltpu.VMEM((N_META, D_RG), _F32),
            pltpu.VMEM((N_META, D_RG), _F32),
            pltpu.VMEM((N_META, D_RG), _F32),
        ],
        name="meta_state",
    )(*args)


def _mixer_kernel(p_ref, h0_ref, tail0_ref, st0_ref, cw_ref, cb_ref, wg_ref, br_ref, bi_ref, lam_ref,
                  rgn_ref, lb_ref, hgn_ref, y_ref, xbuf, acum, bcum, hs, h_s, st_s):
    @pl.when(pl.program_id(1) == 0)
    def _():
        h_s[...] = h0_ref[...]
        xbuf[0:SUBLANES, :] = tail0_ref[...]
        st_s[...] = st0_ref[...]

    n = MIX_ROWS
    prm = (cw_ref[...], cb_ref[...], wg_ref[...], br_ref[...], bi_ref[...], lam_ref[...], rgn_ref[...])
    y_rg = _rg_block(p_ref[0, :, 0:D_RG], p_ref[0, :, D_RG:2 * D_RG], xbuf, acum, bcum, hs, h_s, prm)
    y_ref[0, :, 0:D_RG] = y_rg.astype(y_ref.dtype)

    lb = _hg_lower_bound(lb_ref[...])
    hgn = hgn_ref[...]
    base = 2 * D_RG

    def chunk(ci, carry):
        rows = pl.ds(pl.multiple_of(ci * HG_CHUNK, HG_CHUNK), HG_CHUNK)
        for hd in range(HG_HEADS):
            lo = hd * HG_HEAD_DIM
            col = lambda part: pl.ds(base + part * D_HG + lo, HG_HEAD_DIM)
            o = _hg_chunk(p_ref[0, rows, col(0)], p_ref[0, rows, col(1)], p_ref[0, rows, col(2)],
                          lb[:, lo:lo + HG_HEAD_DIM], st_s, hd)
            o = _rms(o, hgn) * _silu(p_ref[0, rows, col(3)])
            y_ref[0, rows, pl.ds(D_RG + lo, HG_HEAD_DIM)] = o.astype(y_ref.dtype)
        return carry

    lax.fori_loop(0, n // HG_CHUNK, chunk, 0)


def _mixer(p, h0, tail0, st0, cw, cb, wg, br, bi, lam, rgn, lb2, hgn):
    b, s, _ = p.shape
    n = MIX_ROWS
    const = lambda a: pl.BlockSpec(a.shape, lambda bi_, ti_, nd=a.ndim: (0,) * nd)
    small = (h0, tail0, st0, cw, cb, wg, br, bi, lam, rgn, lb2, hgn)
    return pl.pallas_call(
        _mixer_kernel,
        grid=(b, s // n),
        in_specs=[pl.BlockSpec((1, n, D_IN), lambda bi_, ti_: (bi_, ti_, 0))] + [const(a) for a in small],
        out_specs=pl.BlockSpec((1, n, D_MODEL), lambda bi_, ti_: (bi_, ti_, 0)),
        out_shape=jax.ShapeDtypeStruct((b, s, D_MODEL), _BF16),
        scratch_shapes=[
            pltpu.VMEM((n + SUBLANES, D_RG), _F32),
            pltpu.VMEM((n, D_RG), _F32),
            pltpu.VMEM((n, D_RG), _F32),
            pltpu.VMEM((n, D_RG), _F32),
            pltpu.VMEM((SUBLANES, D_RG), _F32),
            pltpu.VMEM((HG_HEADS, HG_HEAD_DIM, HG_HEAD_DIM), _F32),
        ],
        compiler_params=pltpu.CompilerParams(
            dimension_semantics=("arbitrary", "arbitrary"), vmem_limit_bytes=VMEM_LIMIT),
        name="mixer",
    )(p, *small)


def _out_ffn_kernel(x_ref, y_ref, wo_ref, fg_ref, wgu_ref, wd_ref, ng_ref, o_ref, acc):
    h1 = x_ref[...] + jnp.dot(y_ref[...], wo_ref[...], preferred_element_type=_F32)
    v = _rms(h1, fg_ref[...]).astype(_BF16)
    acc[...] = h1
    for j in range(D_FF // FF_TILE):
        lo = j * FF_TILE
        gate = jnp.dot(v, wgu_ref[:, lo:lo + FF_TILE], preferred_element_type=_F32)
        up = jnp.dot(v, wgu_ref[:, D_FF + lo:D_FF + lo + FF_TILE], preferred_element_type=_F32)
        act = (_silu(gate) * up).astype(_BF16)
        acc[...] += jnp.dot(act, wd_ref[lo:lo + FF_TILE, :], preferred_element_type=_F32)
    o_ref[...] = _rms(acc[...], ng_ref[...])


def _out_ffn(x2d, y2d, wo, fg, wgu, wd, ng, rows):
    t, d = x2d.shape
    const = lambda a: pl.BlockSpec(a.shape, lambda i, nd=a.ndim: (0,) * nd, pipeline_mode=pl.Buffered(1))
    return pl.pallas_call(
        _out_ffn_kernel,
        grid=(t // rows,),
        in_specs=[
            pl.BlockSpec((rows, d), lambda i: (i, 0)),
            pl.BlockSpec((rows, d), lambda i: (i, 0)),
            const(wo), const(fg), const(wgu), const(wd), const(ng),
        ],
        out_specs=pl.BlockSpec((rows, d), lambda i: (i, 0)),
        out_shape=jax.ShapeDtypeStruct((t, d), _F32),
        scratch_shapes=[pltpu.VMEM((rows, d), _F32)],
        compiler_params=pltpu.CompilerParams(
            dimension_semantics=("arbitrary",), vmem_limit_bytes=VMEM_LIMIT),
        name="out_ffn",
    )(x2d, y2d, wo, fg, wgu, wd, ng)


def _block_diag(w):
    h, d, _ = w.shape
    eye = jnp.eye(h, dtype=w.dtype)
    return (eye[:, None, :, None] * w[:, :, None, :]).reshape(h * d, h * d)


def kernel(x, meta_tokens, mix_norm_g, w_in, conv_w, conv_b, w_rgate, b_rgate, w_igate, b_igate,
           lru_lambda, rg_norm_g, hg_lower_bound, hg_norm_g, w_out, ffn_norm_g, w_gate_up, w_down,
           final_norm_g):
    b, s, d = x.shape
    row = lambda a: a.reshape(1, -1).astype(_F32)
    w_in_b = w_in[0].astype(_BF16)
    wg = jnp.concatenate([_block_diag(w_rgate[0]), _block_diag(w_igate[0])], axis=1).astype(_BF16)
    mixg = row(mix_norm_g[0])
    mix_prm = (conv_w[0].astype(_F32), row(conv_b[0]), wg, row(b_rgate[0]), row(b_igate[0]),
               row(lru_lambda[0]), row(rg_norm_g[0]), hg_lower_bound.astype(_F32))

    x2d = x.reshape(b * s, d)
    p_meta = _norm_proj(meta_tokens.astype(x.dtype), mixg, w_in_b, N_META)
    h0, tail0, st0 = _meta_state(p_meta, *mix_prm)
    p = _norm_proj(x2d, mixg, w_in_b, TOK_ROWS).reshape(b, s, D_IN)
    y = _mixer(p, h0, tail0, st0, *mix_prm, row(hg_norm_g[0]))
    out = _out_ffn(x2d, y.reshape(b * s, d), w_out[0].astype(_BF16), row(ffn_norm_g[0]),
                   w_gate_up[0].astype(_BF16), w_down[0].astype(_BF16), row(final_norm_g), TOK_ROWS)
    return out.reshape(b, s, d)
```

```python
import functools

import jax
import jax.numpy as jnp
from jax import lax
from jax.experimental import pallas as pl
from jax.experimental.pallas import tpu as pltpu

D_MODEL = 1024
N_META = 16
D_RG = 512
RG_HEADS = 8
RG_HEAD_DIM = 64
CONV_W = 4
LRU_C = 8.0
D_HG = 512
HG_HEAD_DIM = 128
HG_HEADS = 4
D_IN = 2 * D_RG + 4 * D_HG
D_FF = 2816
EPS = 1e-6

HG_CHUNK = 64
SUBLANES = 8
MIX_ROWS = 256
TOK_ROWS = 512
FF_TILE = 256
VMEM_LIMIT = 56 * 1024 * 1024

_F32 = jnp.float32
_BF16 = jnp.bfloat16
_NT = (((1,), (1,)), ((), ()))
_TN = (((0,), (0,)), ((), ()))


def _rms(x, g):
    return x * lax.rsqrt(jnp.mean(x * x, axis=-1, keepdims=True) + EPS) * g


def _sigmoid(x):
    return 0.5 * jnp.tanh(0.5 * x) + 0.5


def _silu(x):
    return x * _sigmoid(x)


def _gelu_tanh(x):
    c = 0.7978845608028654
    t = jnp.tanh(x * (c + (c * 0.044715) * (x * x)))
    return (0.5 * x) * (1.0 + t)


def _norm_proj_kernel(x_ref, g_ref, w_ref, o_ref):
    u = _rms(x_ref[...], g_ref[...]).astype(_BF16)
    o_ref[...] = jnp.dot(u, w_ref[...], preferred_element_type=_F32)


def _norm_proj(x2d, g, w_bf16, rows):
    t, d = x2d.shape
    n = w_bf16.shape[1]
    return pl.pallas_call(
        _norm_proj_kernel,
        grid=(t // rows,),
        in_specs=[
            pl.BlockSpec((rows, d), lambda i: (i, 0)),
            pl.BlockSpec((1, d), lambda i: (0, 0)),
            pl.BlockSpec((d, n), lambda i: (0, 0)),
        ],
        out_specs=pl.BlockSpec((rows, n), lambda i: (i, 0)),
        out_shape=jax.ShapeDtypeStruct((t, n), _F32),
        compiler_params=pltpu.CompilerParams(
            dimension_semantics=("arbitrary",), vmem_limit_bytes=VMEM_LIMIT),
        name="norm_proj",
    )(x2d, g, w_bf16)


def _rg_block(xr, gr, xbuf, acum, bcum, hs, h_ref, prm):
    conv_w, conv_b, w_gate, b_r, b_i, lam, norm_g = prm
    n = xr.shape[0]
    xbuf[SUBLANES:SUBLANES + n, :] = xr
    xc = conv_b + conv_w[3:4, :] * xr
    for j in range(1, CONV_W):
        xc = xc + conv_w[CONV_W - 1 - j:CONV_W - j, :] * xbuf[SUBLANES - j:SUBLANES - j + n, :]
    xbuf[0:SUBLANES, :] = xr[n - SUBLANES:n, :]

    gates = jnp.dot(xc.astype(_BF16), w_gate, preferred_element_type=_F32)
    r = _sigmoid(gates[:, :D_RG] + b_r)
    i = _sigmoid(gates[:, D_RG:] + b_i)
    z = -lam
    softplus = jnp.maximum(z, 0.0) + jnp.log1p(jnp.exp(-jnp.abs(z)))
    log_a = (-LRU_C * softplus) * r
    a = jnp.exp(log_a)
    bx = jnp.sqrt(-jnp.tanh(log_a) * (a * a + 1.0)) * (i * xc)

    pos = lax.broadcasted_iota(jnp.int32, (n, 1), 0) & (SUBLANES - 1)
    for s in (1, 2, 4):
        keep = pos >= s
        a_prev = pltpu.roll(a, s, 0)
        b_prev = pltpu.roll(bx, s, 0)
        bx = jnp.where(keep, a * b_prev + bx, bx)
        a = jnp.where(keep, a * a_prev, a)
    acum[0:n, :] = a
    bcum[0:n, :] = bx

    def group(g, h):
        rows = pl.ds(pl.multiple_of(g * SUBLANES, SUBLANES), SUBLANES)
        hg = acum[rows, :] * h + bcum[rows, :]
        hs[rows, :] = hg
        return jnp.broadcast_to(hg[SUBLANES - 1:SUBLANES, :], (SUBLANES, D_RG))

    h_ref[...] = lax.fori_loop(0, n // SUBLANES, group, h_ref[...])
    y = _gelu_tanh(gr) * hs[0:n, :]
    return _rms(y, norm_g)


def _hg_chunk(hq, hf, hi, lb, st_ref, head, valid=None):
    c = HG_CHUNK
    q = _silu(hq)
    f = lb + (1.0 - lb) * _sigmoid(hf)
    v = hi
    if valid is not None:
        q = jnp.where(valid, q, 0.0)
        f = jnp.where(valid, f, 1.0)
        v = jnp.where(valid, v, 0.0)
    k = 1.0 - f
    vb = v.astype(_BF16)

    row = lax.broadcasted_iota(jnp.int32, (c, 1), 0)
    ti = lax.broadcasted_iota(jnp.int32, (c, c), 0)
    si = lax.broadcasted_iota(jnp.int32, (c, c), 1)
    att = jnp.where(ti == si,
                    lax.dot_general(q.astype(_BF16), k.astype(_BF16), _NT, preferred_element_type=_F32),
                    0.0)
    def same_block(prod, m):
        if 2 * m == c:
            return prod
        shift = (2 * m).bit_length() - 1
        return jnp.where((ti >> shift) == (si >> shift), prod, 0.0)

    tot = f
    pre = f
    suf = jnp.ones_like(f)
    m = 1
    while m < SUBLANES:
        upper = (row & m) != 0
        lhs = jnp.where(upper, q * pre, 0.0).astype(_BF16)
        rhs = jnp.where(upper, 0.0, k * suf).astype(_BF16)
        att = att + same_block(lax.dot_general(lhs, rhs, _NT, preferred_element_type=_F32), m)
        sib = jnp.where(upper, pltpu.roll(tot, m, 0), pltpu.roll(tot, c - m, 0))
        pre = jnp.where(upper, pre * sib, pre)
        suf = jnp.where(upper, suf, suf * sib)
        tot = tot * sib
        m *= 2

    ns = c // SUBLANES
    slabs = lambda a: [a[i * SUBLANES:(i + 1) * SUBLANES] for i in range(ns)]
    q_s, k_s, pre_s, suf_s, tot_s = slabs(q), slabs(k), slabs(pre), slabs(suf), slabs(tot)
    zero = jnp.zeros((SUBLANES, q.shape[1]), _F32)
    g = 1
    while g < ns:
        lhs = jnp.concatenate([q_s[i] * pre_s[i] if i & g else zero for i in range(ns)], axis=0)
        rhs = jnp.concatenate([zero if i & g else k_s[i] * suf_s[i] for i in range(ns)], axis=0)
        prod = lax.dot_general(lhs.astype(_BF16), rhs.astype(_BF16), _NT, preferred_element_type=_F32)
        att = att + same_block(prod, g * SUBLANES)
        sib = [tot_s[i ^ g] for i in range(ns)]
        pre_s = [pre_s[i] * sib[i] if i & g else pre_s[i] for i in range(ns)]
        suf_s = [suf_s[i] if i & g else suf_s[i] * sib[i] for i in range(ns)]
        tot_s = [tot_s[i] * sib[i] for i in range(ns)]
        g *= 2
    pre = jnp.concatenate(pre_s, axis=0)
    suf = jnp.concatenate(suf_s, axis=0)
    tot = tot_s[0]

    st = st_ref[head]
    o = lax.dot_general((q * pre).astype(_BF16), st.astype(_BF16), _NT, preferred_element_type=_F32)
    o = o + jnp.dot(att.astype(_BF16), vb, preferred_element_type=_F32)
    upd = lax.dot_general(vb, (k * suf).astype(_BF16), _TN, preferred_element_type=_F32)
    st_ref[head] = st * tot[0:1, :] + upd
    return o


def _hg_lower_bound(lb2):
    m = jnp.maximum(lb2[0:1, :], lb2[1:2, :])
    e0 = jnp.exp(lb2[0:1, :] - m)
    e1 = jnp.exp(lb2[1:2, :] - m)
    return e0 / (e0 + e1)


def _meta_state_kernel(p_ref, cw_ref, cb_ref, wg_ref, br_ref, bi_ref, lam_ref, rgn_ref, lb_ref,
                       h_out, tail_out, st_out, xbuf, acum, bcum, hs):
    n = N_META
    xbuf[...] = jnp.zeros_like(xbuf)
    h_out[...] = jnp.zeros_like(h_out)
    prm = (cw_ref[...], cb_ref[...], wg_ref[...], br_ref[...], bi_ref[...], lam_ref[...], rgn_ref[...])
    _rg_block(p_ref[:, 0:D_RG], p_ref[:, D_RG:2 * D_RG], xbuf, acum, bcum, hs, h_out, prm)
    tail_out[...] = xbuf[0:SUBLANES, :]

    st_out[...] = jnp.zeros_like(st_out)
    lb = _hg_lower_bound(lb_ref[...])
    pad = HG_CHUNK - n
    valid = lax.broadcasted_iota(jnp.int32, (HG_CHUNK, 1), 0) >= pad
    zeros = jnp.zeros((pad, HG_HEAD_DIM), _F32)
    base = 2 * D_RG
    for hd in range(HG_HEADS):
        lo = hd * HG_HEAD_DIM
        cols = lambda part: slice(base + part * D_HG + lo, base + part * D_HG + lo + HG_HEAD_DIM)
        hq = jnp.concatenate([zeros, p_ref[:, cols(0)]], axis=0)
        hf = jnp.concatenate([zeros, p_ref[:, cols(1)]], axis=0)
        hi = jnp.concatenate([zeros, p_ref[:, cols(2)]], axis=0)
        _hg_chunk(hq, hf, hi, lb[:, lo:lo + HG_HEAD_DIM], st_out, hd, valid=valid)


def _meta_state(p_meta, cw, cb, wg, br, bi, lam, rgn, lb2):
    full = lambda a: pl.BlockSpec(a.shape, lambda: (0,) * a.ndim)
    args = (p_meta, cw, cb, wg, br, bi, lam, rgn, lb2)
    out_shapes = (
        jax.ShapeDtypeStruct((SUBLANES, D_RG), _F32),
        jax.ShapeDtypeStruct((SUBLANES, D_RG), _F32),
        jax.ShapeDtypeStruct((HG_HEADS, HG_HEAD_DIM, HG_HEAD_DIM), _F32),
    )
    return pl.pallas_call(
        _meta_state_kernel,
        in_specs=[full(a) for a in args],
        out_specs=tuple(pl.BlockSpec(s.shape, lambda n=len(s.shape): (0,) * n) for s in out_shapes),
        out_shape=out_shapes,
        scratch_shapes=[
            pltpu.VMEM((N_META + SUBLANES, D_RG), _F32),
            pltpu.VMEM((N_META, D_RG), _F32),
            pltpu.VMEM((N_META, D_RG), _F32),
            pltpu.VMEM((N_META, D_RG), _F32),
        ],
        name="meta_state",
    )(*args)


def _mixer_kernel(p_ref, h0_ref, tail0_ref, st0_ref, cw_ref, cb_ref, wg_ref, br_ref, bi_ref, lam_ref,
                  rgn_ref, lb_ref, hgn_ref, y_ref, xbuf, acum, bcum, hs, h_s, st_s):
    @pl.when(pl.program_id(1) == 0)
    def _():
        h_s[...] = h0_ref[...]
        xbuf[0:SUBLANES, :] = tail0_ref[...]
        st_s[...] = st0_ref[...]

    n = MIX_ROWS
    prm = (cw_ref[...], cb_ref[...], wg_ref[...], br_ref[...], bi_ref[...], lam_ref[...], rgn_ref[...])
    y_rg = _rg_block(p_ref[0, :, 0:D_RG], p_ref[0, :, D_RG:2 * D_RG], xbuf, acum, bcum, hs, h_s, prm)
    y_ref[0, :, 0:D_RG] = y_rg.astype(y_ref.dtype)

    lb = _hg_lower_bound(lb_ref[...])
    hgn = hgn_ref[...]
    base = 2 * D_RG

    def chunk(ci, carry):
        rows = pl.ds(pl.multiple_of(ci * HG_CHUNK, HG_CHUNK), HG_CHUNK)
        for hd in range(HG_HEADS):
            lo = hd * HG_HEAD_DIM
            col = lambda part: pl.ds(base + part * D_HG + lo, HG_HEAD_DIM)
            o = _hg_chunk(p_ref[0, rows, col(0)], p_ref[0, rows, col(1)], p_ref[0, rows, col(2)],
                          lb[:, lo:lo + HG_HEAD_DIM], st_s, hd)
            o = _rms(o, hgn) * _silu(p_ref[0, rows, col(3)])
            y_ref[0, rows, pl.ds(D_RG + lo, HG_HEAD_DIM)] = o.astype(y_ref.dtype)
        return carry

    lax.fori_loop(0, n // HG_CHUNK, chunk, 0)


def _mixer(p, h0, tail0, st0, cw, cb, wg, br, bi, lam, rgn, lb2, hgn):
    b, s, _ = p.shape
    n = MIX_ROWS
    const = lambda a: pl.BlockSpec(a.shape, lambda bi_, ti_, nd=a.ndim: (0,) * nd)
    small = (h0, tail0, st0, cw, cb, wg, br, bi, lam, rgn, lb2, hgn)
    return pl.pallas_call(
        _mixer_kernel,
        grid=(b, s // n),
        in_specs=[pl.BlockSpec((1, n, D_IN), lambda bi_, ti_: (bi_, ti_, 0))] + [const(a) for a in small],
        out_specs=pl.BlockSpec((1, n, D_MODEL), lambda bi_, ti_: (bi_, ti_, 0)),
        out_shape=jax.ShapeDtypeStruct((b, s, D_MODEL), _BF16),
        scratch_shapes=[
            pltpu.VMEM((n + SUBLANES, D_RG), _F32),
            pltpu.VMEM((n, D_RG), _F32),
            pltpu.VMEM((n, D_RG), _F32),
            pltpu.VMEM((n, D_RG), _F32),
            pltpu.VMEM((SUBLANES, D_RG), _F32),
            pltpu.VMEM((HG_HEADS, HG_HEAD_DIM, HG_HEAD_DIM), _F32),
        ],
        compiler_params=pltpu.CompilerParams(
            dimension_semantics=("arbitrary", "arbitrary"), vmem_limit_bytes=VMEM_LIMIT),
        name="mixer",
    )(p, *small)


def _out_ffn_kernel(x_ref, y_ref, wo_ref, fg_ref, wgu_ref, wd_ref, ng_ref, o_ref, acc):
    h1 = x_ref[...] + jnp.dot(y_ref[...], wo_ref[...], preferred_element_type=_F32)
    v = _rms(h1, fg_ref[...]).astype(_BF16)
    acc[...] = h1
    for j in range(D_FF // FF_TILE):
        lo = j * FF_TILE
        gate = jnp.dot(v, wgu_ref[:, lo:lo + FF_TILE], preferred_element_type=_F32)
        up = jnp.dot(v, wgu_ref[:, D_FF + lo:D_FF + lo + FF_TILE], preferred_element_type=_F32)
        act = (_silu(gate) * up).astype(_BF16)
        acc[...] += jnp.dot(act, wd_ref[lo:lo + FF_TILE, :], preferred_element_type=_F32)
    o_ref[...] = _rms(acc[...], ng_ref[...])


def _out_ffn(x2d, y2d, wo, fg, wgu, wd, ng, rows):
    t, d = x2d.shape
    const = lambda a: pl.BlockSpec(a.shape, lambda i, nd=a.ndim: (0,) * nd, pipeline_mode=pl.Buffered(1))
    return pl.pallas_call(
        _out_ffn_kernel,
        grid=(t // rows,),
        in_specs=[
            pl.BlockSpec((rows, d), lambda i: (i, 0)),
            pl.BlockSpec((rows, d), lambda i: (i, 0)),
            const(wo), const(fg), const(wgu), const(wd), const(ng),
        ],
        out_specs=pl.BlockSpec((rows, d), lambda i: (i, 0)),
        out_shape=jax.ShapeDtypeStruct((t, d), _F32),
        scratch_shapes=[pltpu.VMEM((rows, d), _F32)],
        compiler_params=pltpu.CompilerParams(
            dimension_semantics=("arbitrary",), vmem_limit_bytes=VMEM_LIMIT),
        name="out_ffn",
    )(x2d, y2d, wo, fg, wgu, wd, ng)


def _block_diag(w):
    h, d, _ = w.shape
    eye = jnp.eye(h, dtype=w.dtype)
    return (eye[:, None, :, None] * w[:, :, None, :]).reshape(h * d, h * d)


def kernel(x, meta_tokens, mix_norm_g, w_in, conv_w, conv_b, w_rgate, b_rgate, w_igate, b_igate,
           lru_lambda, rg_norm_g, hg_lower_bound, hg_norm_g, w_out, ffn_norm_g, w_gate_up, w_down,
           final_norm_g):
    b, s, d = x.shape
    row = lambda a: a.reshape(1, -1).astype(_F32)
    w_in_b = w_in[0].astype(_BF16)
    wg = jnp.concatenate([_block_diag(w_rgate[0]), _block_diag(w_igate[0])], axis=1).astype(_BF16)
    mixg = row(mix_norm_g[0])
    mix_prm = (conv_w[0].astype(_F32), row(conv_b[0]), wg, row(b_rgate[0]), row(b_igate[0]),
               row(lru_lambda[0]), row(rg_norm_g[0]), hg_lower_bound.astype(_F32))

    x2d = x.reshape(b * s, d)
    p_meta = _norm_proj(meta_tokens.astype(x.dtype), mixg, w_in_b, N_META)
    h0, tail0, st0 = _meta_state(p_meta, *mix_prm)
    p = _norm_proj(x2d, mixg, w_in_b, TOK_ROWS).reshape(b, s, D_IN)
    y = _mixer(p, h0, tail0, st0, *mix_prm, row(hg_norm_g[0]))
    out = _out_ffn(x2d, y.reshape(b * s, d), w_out[0].astype(_BF16), row(ffn_norm_g[0]),
                   w_gate_up[0].astype(_BF16), w_down[0].astype(_BF16), row(final_norm_g), TOK_ROWS)
    return out.reshape(b, s, d)
```

```python
import functools

import jax
import jax.numpy as jnp
from jax import lax
from jax.experimental import pallas as pl
from jax.experimental.pallas import tpu as pltpu

D_MODEL = 1024
N_META = 16
D_RG = 512
CONV_W = 4
LRU_C = 8.0
D_HG = 512
HG_HEAD_DIM = 128
HG_HEADS = 4
D_IN = 2 * D_RG + 4 * D_HG
D_FF = 2816
EPS = 1e-6

HG_CHUNK = 64
SUBLANES = 8
BLOCK_ROWS = 256
IN_TILE = 512
FF_TILE = 256
VMEM_LIMIT = 60 * 1024 * 1024

_F32 = jnp.float32
_BF16 = jnp.bfloat16
_NT = (((1,), (1,)), ((), ()))
_TN = (((0,), (0,)), ((), ()))


def _rms(x, g):
    return x * lax.rsqrt(jnp.mean(x * x, axis=-1, keepdims=True) + EPS) * g


def _sigmoid(x):
    return 0.5 * jnp.tanh(0.5 * x) + 0.5


def _silu(x):
    return x * _sigmoid(x)


def _gelu_tanh(x):
    c = 0.7978845608028654
    t = jnp.tanh(x * (c + (c * 0.044715) * (x * x)))
    return (0.5 * x) * (1.0 + t)


def _dot(a, b):
    return jnp.dot(a, b, preferred_element_type=_F32)


def _rg_block(xr, gr, xbuf, h_ref, prm):
    conv_w, conv_b, w_gate, b_r, b_i, lam, norm_g = prm
    n = xr.shape[0]
    xbuf[SUBLANES:SUBLANES + n, :] = xr
    xc = conv_b + conv_w[3:4, :] * xr
    for j in range(1, CONV_W):
        xc = xc + conv_w[CONV_W - 1 - j:CONV_W - j, :] * xbuf[SUBLANES - j:SUBLANES - j + n, :]
    xbuf[0:SUBLANES, :] = xr[n - SUBLANES:n, :]

    gates = _dot(xc.astype(_BF16), w_gate)
    r = _sigmoid(gates[:, :D_RG] + b_r)
    i = _sigmoid(gates[:, D_RG:] + b_i)
    z = -lam
    softplus = jnp.maximum(z, 0.0) + jnp.log1p(jnp.exp(-jnp.abs(z)))
    log_a = (-LRU_C * softplus) * r
    a = jnp.exp(log_a)
    bx = jnp.sqrt(-jnp.tanh(log_a) * (a * a + 1.0)) * (i * xc)

    pos = lax.broadcasted_iota(jnp.int32, (n, 1), 0) & (SUBLANES - 1)
    for s in (1, 2, 4):
        keep = pos >= s
        a_prev = pltpu.roll(a, s, 0)
        b_prev = pltpu.roll(bx, s, 0)
        bx = jnp.where(keep, a * b_prev + bx, bx)
        a = jnp.where(keep, a * a_prev, a)
    h = h_ref[...]
    hs = []
    for g in range(n // SUBLANES):
        rows = slice(g * SUBLANES, (g + 1) * SUBLANES)
        hg = a[rows] * h + bx[rows]
        hs.append(hg)
        h = jnp.broadcast_to(hg[SUBLANES - 1:SUBLANES, :], (SUBLANES, D_RG))
    h_ref[...] = h
    y = _gelu_tanh(gr) * jnp.concatenate(hs, axis=0)
    return _rms(y, norm_g)


def _hg_chunk(hq, hf, hi, lb, st_ref, head, valid=None):
    c = HG_CHUNK
    q = _silu(hq)
    f = lb + (1.0 - lb) * _sigmoid(hf)
    v = hi
    if valid is not None:
        q = jnp.where(valid, q, 0.0)
        f = jnp.where(valid, f, 1.0)
        v = jnp.where(valid, v, 0.0)
    k = 1.0 - f
    vb = v.astype(_BF16)

    row = lax.broadcasted_iota(jnp.int32, (c, 1), 0)
    ti = lax.broadcasted_iota(jnp.int32, (c, c), 0)
    si = lax.broadcasted_iota(jnp.int32, (c, c), 1)
    att = jnp.where(ti == si,
                    lax.dot_general(q.astype(_BF16), k.astype(_BF16), _NT, preferred_element_type=_F32),
                    0.0)
    def same_block(prod, m):
        if 2 * m == c:
            return prod
        shift = (2 * m).bit_length() - 1
        return jnp.where((ti >> shift) == (si >> shift), prod, 0.0)

    tot = f
    pre = f
    suf = jnp.ones_like(f)
    m = 1
    while m < SUBLANES:
        upper = (row & m) != 0
        lhs = jnp.where(upper, q * pre, 0.0).astype(_BF16)
        rhs = jnp.where(upper, 0.0, k * suf).astype(_BF16)
        att = att + same_block(lax.dot_general(lhs, rhs, _NT, preferred_element_type=_F32), m)
        sib = jnp.where(upper, pltpu.roll(tot, m, 0), pltpu.roll(tot, c - m, 0))
        pre = jnp.where(upper, pre * sib, pre)
        suf = jnp.where(upper, suf, suf * sib)
        tot = tot * sib
        m *= 2

    ns = c // SUBLANES
    slabs = lambda a: [a[i * SUBLANES:(i + 1) * SUBLANES] for i in range(ns)]
    q_s, k_s, pre_s, suf_s, tot_s = slabs(q), slabs(k), slabs(pre), slabs(suf), slabs(tot)
    zero = jnp.zeros((SUBLANES, q.shape[1]), _F32)
    g = 1
    while g < ns:
        lhs = jnp.concatenate([q_s[i] * pre_s[i] if i & g else zero for i in range(ns)], axis=0)
        rhs = jnp.concatenate([zero if i & g else k_s[i] * suf_s[i] for i in range(ns)], axis=0)
        prod = lax.dot_general(lhs.astype(_BF16), rhs.astype(_BF16), _NT, preferred_element_type=_F32)
        att = att + same_block(prod, g * SUBLANES)
        sib = [tot_s[i ^ g] for i in range(ns)]
        pre_s = [pre_s[i] * sib[i] if i & g else pre_s[i] for i in range(ns)]
        suf_s = [suf_s[i] if i & g else suf_s[i] * sib[i] for i in range(ns)]
        tot_s = [tot_s[i] * sib[i] for i in range(ns)]
        g *= 2
    pre = jnp.concatenate(pre_s, axis=0)
    suf = jnp.concatenate(suf_s, axis=0)
    tot = tot_s[0]

    st = st_ref[head]
    o = lax.dot_general((q * pre).astype(_BF16), st.astype(_BF16), _NT, preferred_element_type=_F32)
    o = o + _dot(att.astype(_BF16), vb)
    upd = lax.dot_general(vb, (k * suf).astype(_BF16), _TN, preferred_element_type=_F32)
    st_ref[head] = st * tot[0:1, :] + upd
    return o


def _hg_lower_bound(lb2):
    m = jnp.maximum(lb2[0:1, :], lb2[1:2, :])
    e0 = jnp.exp(lb2[0:1, :] - m)
    e1 = jnp.exp(lb2[1:2, :] - m)
    return e0 / (e0 + e1)


def _mixer_block(p_ref, y_ref, xbuf, h_s, st_s, prm, lb, hgn):
    n = p_ref.shape[0]
    y_rg = _rg_block(p_ref[:, 0:D_RG], p_ref[:, D_RG:2 * D_RG], xbuf, h_s, prm)
    y_ref[:, 0:D_RG] = y_rg.astype(y_ref.dtype)
    base = 2 * D_RG
    for ci in range(n // HG_CHUNK):
        rows = slice(ci * HG_CHUNK, (ci + 1) * HG_CHUNK)
        for hd in range(HG_HEADS):
            lo = hd * HG_HEAD_DIM
            col = lambda part: slice(base + part * D_HG + lo, base + part * D_HG + lo + HG_HEAD_DIM)
            o = _hg_chunk(p_ref[rows, col(0)], p_ref[rows, col(1)], p_ref[rows, col(2)],
                          lb[:, lo:lo + HG_HEAD_DIM], st_s, hd)
            o = _rms(o, hgn) * _silu(p_ref[rows, col(3)])
            y_ref[rows, D_RG + lo:D_RG + lo + HG_HEAD_DIM] = o.astype(y_ref.dtype)


def _norm_proj_kernel(x_ref, g_ref, w_ref, o_ref):
    u = _rms(x_ref[...], g_ref[...]).astype(_BF16)
    o_ref[...] = _dot(u, w_ref[...])


def _norm_proj(x2d, g, w_bf16):
    t, d = x2d.shape
    n = w_bf16.shape[1]
    full = lambda a: pl.BlockSpec(a.shape, lambda: (0,) * a.ndim)
    return pl.pallas_call(
        _norm_proj_kernel,
        in_specs=[full(x2d), full(g), full(w_bf16)],
        out_specs=pl.BlockSpec((t, n), lambda: (0, 0)),
        out_shape=jax.ShapeDtypeStruct((t, n), _F32),
        compiler_params=pltpu.CompilerParams(vmem_limit_bytes=VMEM_LIMIT),
        name="meta_proj",
    )(x2d, g, w_bf16)


def _meta_state_kernel(p_ref, cw_ref, cb_ref, wg_ref, br_ref, bi_ref, lam_ref, rgn_ref, lb_ref,
                       h_out, tail_out, st_out, xbuf):
    n = N_META
    xbuf[...] = jnp.zeros_like(xbuf)
    h_out[...] = jnp.zeros_like(h_out)
    prm = (cw_ref[...], cb_ref[...], wg_ref[...], br_ref[...], bi_ref[...], lam_ref[...], rgn_ref[...])
    _rg_block(p_ref[:, 0:D_RG], p_ref[:, D_RG:2 * D_RG], xbuf, h_out, prm)
    tail_out[...] = xbuf[0:SUBLANES, :]

    st_out[...] = jnp.zeros_like(st_out)
    lb = _hg_lower_bound(lb_ref[...])
    pad = HG_CHUNK - n
    valid = lax.broadcasted_iota(jnp.int32, (HG_CHUNK, 1), 0) >= pad
    zeros = jnp.zeros((pad, HG_HEAD_DIM), _F32)
    base = 2 * D_RG
    for hd in range(HG_HEADS):
        lo = hd * HG_HEAD_DIM
        cols = lambda part: slice(base + part * D_HG + lo, base + part * D_HG + lo + HG_HEAD_DIM)
        hq = jnp.concatenate([zeros, p_ref[:, cols(0)]], axis=0)
        hf = jnp.concatenate([zeros, p_ref[:, cols(1)]], axis=0)
        hi = jnp.concatenate([zeros, p_ref[:, cols(2)]], axis=0)
        _hg_chunk(hq, hf, hi, lb[:, lo:lo + HG_HEAD_DIM], st_out, hd, valid=valid)


def _meta_state(p_meta, cw, cb, wg, br, bi, lam, rgn, lb2):
    full = lambda a: pl.BlockSpec(a.shape, lambda: (0,) * a.ndim)
    args = (p_meta, cw, cb, wg, br, bi, lam, rgn, lb2)
    out_shapes = (
        jax.ShapeDtypeStruct((SUBLANES, D_RG), _F32),
        jax.ShapeDtypeStruct((SUBLANES, D_RG), _F32),
        jax.ShapeDtypeStruct((HG_HEADS, HG_HEAD_DIM, HG_HEAD_DIM), _F32),
    )
    return pl.pallas_call(
        _meta_state_kernel,
        in_specs=[full(a) for a in args],
        out_specs=tuple(pl.BlockSpec(s.shape, lambda n=len(s.shape): (0,) * n) for s in out_shapes),
        out_shape=out_shapes,
        scratch_shapes=[pltpu.VMEM((N_META + SUBLANES, D_RG), _F32)],
        name="meta_state",
    )(*args)


def _ffn_block(x, y_ref, wo_ref, fg, wgu_ref, wd_ref, ng, act):
    h1 = x + _dot(y_ref[...], wo_ref[...])
    v = _rms(h1, fg).astype(_BF16)
    for j in range(D_FF // FF_TILE):
        lo = j * FF_TILE
        gate = _dot(v, wgu_ref[:, lo:lo + FF_TILE])
        up = _dot(v, wgu_ref[:, D_FF + lo:D_FF + lo + FF_TILE])
        act[:, lo:lo + FF_TILE] = (_silu(gate) * up).astype(_BF16)
    return _rms(h1 + _dot(act[...], wd_ref[...]), ng)


def _layer_kernel(blocks_per_seq,
                  xa_ref, xc_ref, mixg_ref, win_ref, h0_ref, tail0_ref, st0_ref,
                  cw_ref, cb_ref, wg_ref, br_ref, bi_ref, lam_ref, rgn_ref, lb_ref, hgn_ref,
                  wo_ref, fg_ref, wgu_ref, wd_ref, ng_ref,
                  o_ref, p_scr, y_scr, act, xbuf, h_s, st_s):
    j = pl.program_id(0)
    n = BLOCK_ROWS

    @pl.when(j == 0)
    def _():
        p_scr[...] = jnp.zeros_like(p_scr)
        y_scr[...] = jnp.zeros_like(y_scr)
        xbuf[0:SUBLANES, :] = jnp.zeros((SUBLANES, D_RG), _F32)
        h_s[...] = jnp.zeros_like(h_s)
        st_s[...] = jnp.zeros_like(st_s)

    prm = (cw_ref[...], cb_ref[...], wg_ref[...], br_ref[...], bi_ref[...], lam_ref[...], rgn_ref[...])
    lb = _hg_lower_bound(lb_ref[...])
    for half in (0, 1):
        rows = slice(half * n, (half + 1) * n)
        if half == 1:
            @pl.when(lax.rem(2 * j, blocks_per_seq) == 0)
            def _():
                h_s[...] = h0_ref[...]
                xbuf[0:SUBLANES, :] = tail0_ref[...]
                st_s[...] = st0_ref[...]

        u = _rms(xa_ref[rows, :], mixg_ref[...]).astype(_BF16)
        for t in range(D_IN // IN_TILE):
            cols = slice(t * IN_TILE, (t + 1) * IN_TILE)
            p_scr[half, :, cols] = _dot(u, win_ref[:, cols])
        _mixer_block(p_scr.at[1 - half], y_scr.at[1 - half], xbuf, h_s, st_s, prm, lb, hgn_ref[...])
        o_ref[rows, :] = _ffn_block(xc_ref[rows, :], y_scr.at[half], wo_ref, fg_ref[...], wgu_ref, wd_ref,
                                    ng_ref[...], act)


def _layer(x2d, seq, mixg, w_in, h0, tail0, st0, mix_prm, hgn, wo, fg, wgu, wd, ng):
    t, d = x2d.shape
    n = BLOCK_ROWS
    assert t % (2 * n) == 0 and seq % (2 * n) == 0
    steps = t // (2 * n)
    const = lambda a: pl.BlockSpec(a.shape, lambda j, nd=a.ndim: (0,) * nd, pipeline_mode=pl.Buffered(1))
    small = (h0, tail0, st0) + tuple(mix_prm) + (hgn,)
    in_specs = ([pl.BlockSpec((2 * n, d), lambda j: (jnp.minimum(j, steps - 1), 0)),
                 pl.BlockSpec((2 * n, d), lambda j: (jnp.maximum(j - 1, 0), 0)),
                 const(mixg), const(w_in)] + [const(a) for a in small]
                + [const(wo), const(fg), const(wgu), const(wd), const(ng)])
    return pl.pallas_call(
        functools.partial(_layer_kernel, seq // n),
        grid=(steps + 1,),
        in_specs=in_specs,
        out_specs=pl.BlockSpec((2 * n, d), lambda j: (jnp.maximum(j - 1, 0), 0)),
        out_shape=jax.ShapeDtypeStruct((t, d), _F32),
        scratch_shapes=[
            pltpu.VMEM((2, n, D_IN), _F32),
            pltpu.VMEM((2, n, D_MODEL), _BF16),
            pltpu.VMEM((n, D_FF), _BF16),
            pltpu.VMEM((n + SUBLANES, D_RG), _F32),
            pltpu.VMEM((SUBLANES, D_RG), _F32),
            pltpu.VMEM((HG_HEADS, HG_HEAD_DIM, HG_HEAD_DIM), _F32),
        ],
        compiler_params=pltpu.CompilerParams(
            dimension_semantics=("arbitrary",), vmem_limit_bytes=VMEM_LIMIT),
        name="layer",
    )(x2d, x2d, mixg, w_in, *small, wo, fg, wgu, wd, ng)


def _block_diag(w):
    h, d, _ = w.shape
    eye = jnp.eye(h, dtype=w.dtype)
    return (eye[:, None, :, None] * w[:, :, None, :]).reshape(h * d, h * d)


def kernel(x, meta_tokens, mix_norm_g, w_in, conv_w, conv_b, w_rgate, b_rgate, w_igate, b_igate,
           lru_lambda, rg_norm_g, hg_lower_bound, hg_norm_g, w_out, ffn_norm_g, w_gate_up, w_down,
           final_norm_g):
    b, s, d = x.shape
    row = lambda a: a.reshape(1, -1).astype(_F32)
    w_in_b = w_in[0].astype(_BF16)
    wg = jnp.concatenate([_block_diag(w_rgate[0]), _block_diag(w_igate[0])], axis=1).astype(_BF16)
    mixg = row(mix_norm_g[0])
    mix_prm = (conv_w[0].astype(_F32), row(conv_b[0]), wg, row(b_rgate[0]), row(b_igate[0]),
               row(lru_lambda[0]), row(rg_norm_g[0]), hg_lower_bound.astype(_F32))

    p_meta = _norm_proj(meta_tokens.astype(x.dtype), mixg, w_in_b)
    h0, tail0, st0 = _meta_state(p_meta, *mix_prm)
    out = _layer(x.reshape(b * s, d), s, mixg, w_in_b, h0, tail0, st0, mix_prm, row(hg_norm_g[0]),
                 w_out[0].astype(_BF16), row(ffn_norm_g[0]), w_gate_up[0].astype(_BF16),
                 w_down[0].astype(_BF16), row(final_norm_g))
    return out.reshape(b, s, d)
```

```python
import functools

import jax
import jax.numpy as jnp
from jax import lax
from jax.experimental import pallas as pl
from jax.experimental.pallas import tpu as pltpu

D_MODEL = 1024
N_META = 16
D_RG = 512
CONV_W = 4
LRU_C = 8.0
D_HG = 512
HG_HEAD_DIM = 128
HG_HEADS = 4
D_IN = 2 * D_RG + 4 * D_HG
D_FF = 2816
EPS = 1e-6

HG_CHUNK = 64
SUBLANES = 8
BLOCK_ROWS = 256
IN_TILE = 512
FF_TILE = 256
VMEM_LIMIT = 60 * 1024 * 1024

_F32 = jnp.float32
_BF16 = jnp.bfloat16
_NT = (((1,), (1,)), ((), ()))
_TN = (((0,), (0,)), ((), ()))


def _rms(x, g):
    return x * lax.rsqrt(jnp.mean(x * x, axis=-1, keepdims=True) + EPS) * g


def _sigmoid(x):
    return 0.5 * jnp.tanh(0.5 * x) + 0.5


def _silu(x):
    return x * _sigmoid(x)


def _gelu_tanh(x):
    c = 0.7978845608028654
    t = jnp.tanh(x * (c + (c * 0.044715) * (x * x)))
    return (0.5 * x) * (1.0 + t)


def _dot(a, b):
    return jnp.dot(a, b, preferred_element_type=_F32)


def _slab_roll(x, shift):
    groups = [pltpu.roll(x[i:i + SUBLANES], shift, 0) for i in range(0, x.shape[0], SUBLANES)]
    return jnp.concatenate(groups, axis=0)


def _rg_gates(xr, xbuf, prm):
    conv_w, conv_b, w_gate = prm[0], prm[1], prm[2]
    n = xr.shape[0]
    xbuf[SUBLANES:SUBLANES + n, :] = xr
    xc = conv_b + conv_w[3:4, :] * xr
    for j in range(1, CONV_W):
        xc = xc + conv_w[CONV_W - 1 - j:CONV_W - j, :] * xbuf[SUBLANES - j:SUBLANES - j + n, :]
    xbuf[0:SUBLANES, :] = xr[n - SUBLANES:n, :]
    return xc, _dot(xc.astype(_BF16), w_gate)


def _rg_rows(xc, gates, gr, h_ref, prm):
    b_r, b_i, lam, norm_g = prm[3], prm[4], prm[5], prm[6]
    n = xc.shape[0]
    r = _sigmoid(gates[:, :D_RG] + b_r)
    i = _sigmoid(gates[:, D_RG:] + b_i)
    z = -lam
    softplus = jnp.maximum(z, 0.0) + jnp.log1p(jnp.exp(-jnp.abs(z)))
    log_a = (-LRU_C * softplus) * r
    a = jnp.exp(log_a)
    bx = jnp.sqrt(-jnp.tanh(log_a) * (a * a + 1.0)) * (i * xc)

    pos = lax.broadcasted_iota(jnp.int32, (n, 1), 0) & (SUBLANES - 1)
    for s in (1, 2, 4):
        keep = pos >= s
        a_prev = _slab_roll(a, s)
        b_prev = _slab_roll(bx, s)
        bx = jnp.where(keep, a * b_prev + bx, bx)
        a = jnp.where(keep, a * a_prev, a)
    h = h_ref[...]
    hs = []
    for g in range(n // SUBLANES):
        rows = slice(g * SUBLANES, (g + 1) * SUBLANES)
        hg = a[rows] * h + bx[rows]
        hs.append(hg)
        h = jnp.broadcast_to(hg[SUBLANES - 1:SUBLANES, :], (SUBLANES, D_RG))
    h_ref[...] = h
    y = _gelu_tanh(gr) * jnp.concatenate(hs, axis=0)
    return _rms(y, norm_g)


def _hg_chunk(hq, hf, hi, lb, st_ref, head, valid=None):
    c = HG_CHUNK
    q = _silu(hq)
    f = lb + (1.0 - lb) * _sigmoid(hf)
    v = hi
    if valid is not None:
        q = jnp.where(valid, q, 0.0)
        f = jnp.where(valid, f, 1.0)
        v = jnp.where(valid, v, 0.0)
    k = 1.0 - f
    vb = v.astype(_BF16)

    row = lax.broadcasted_iota(jnp.int32, (c, 1), 0)
    pos = row & (SUBLANES - 1)
    ti = lax.broadcasted_iota(jnp.int32, (c, c), 0)
    si = lax.broadcasted_iota(jnp.int32, (c, c), 1)

    o_band = jnp.sum(q * k, axis=-1, keepdims=True) * v
    qw = q
    for d in range(1, SUBLANES):
        qw = qw * (f if d == 1 else _slab_roll(f, d - 1))
        a = jnp.sum(qw * _slab_roll(k, d), axis=-1, keepdims=True)
        o_band = o_band + jnp.where(pos >= d, a, 0.0) * _slab_roll(v, d)

    def same_block(prod, m):
        if 2 * m == c:
            return prod
        shift = (2 * m).bit_length() - 1
        return jnp.where((ti >> shift) == (si >> shift), prod, 0.0)

    tot = f
    pre = f
    suf = jnp.ones_like(f)
    m = 1
    while m < SUBLANES:
        upper = (row & m) != 0
        sib = jnp.where(upper, _slab_roll(tot, m), _slab_roll(tot, SUBLANES - m))
        pre = jnp.where(upper, pre * sib, pre)
        suf = jnp.where(upper, suf, suf * sib)
        tot = tot * sib
        m *= 2

    att = jnp.zeros((c, c), _F32)
    ns = c // SUBLANES
    slabs = lambda a: [a[i * SUBLANES:(i + 1) * SUBLANES] for i in range(ns)]
    q_s, k_s, pre_s, suf_s, tot_s = slabs(q), slabs(k), slabs(pre), slabs(suf), slabs(tot)
    zero = jnp.zeros((SUBLANES, q.shape[1]), _F32)
    g = 1
    while g < ns:
        lhs = jnp.concatenate([q_s[i] * pre_s[i] if i & g else zero for i in range(ns)], axis=0)
        rhs = jnp.concatenate([zero if i & g else k_s[i] * suf_s[i] for i in range(ns)], axis=0)
        prod = lax.dot_general(lhs.astype(_BF16), rhs.astype(_BF16), _NT, preferred_element_type=_F32)
        att = att + same_block(prod, g * SUBLANES)
        sib = [tot_s[i ^ g] for i in range(ns)]
        pre_s = [pre_s[i] * sib[i] if i & g else pre_s[i] for i in range(ns)]
        suf_s = [suf_s[i] if i & g else suf_s[i] * sib[i] for i in range(ns)]
        tot_s = [tot_s[i] * sib[i] for i in range(ns)]
        g *= 2
    pre = jnp.concatenate(pre_s, axis=0)
    suf = jnp.concatenate(suf_s, axis=0)
    tot = tot_s[0]

    st = st_ref[head]
    o = lax.dot_general((q * pre).astype(_BF16), st.astype(_BF16), _NT, preferred_element_type=_F32)
    o = o + _dot(att.astype(_BF16), vb) + o_band
    upd = lax.dot_general(vb, (k * suf).astype(_BF16), _TN, preferred_element_type=_F32)
    st_ref[head] = st * tot[0:1, :] + upd
    return o


def _hg_lower_bound(lb2):
    m = jnp.maximum(lb2[0:1, :], lb2[1:2, :])
    e0 = jnp.exp(lb2[0:1, :] - m)
    e1 = jnp.exp(lb2[1:2, :] - m)
    return e0 / (e0 + e1)


def _norm_proj_kernel(x_ref, g_ref, w_ref, o_ref):
    u = _rms(x_ref[...], g_ref[...]).astype(_BF16)
    o_ref[...] = _dot(u, w_ref[...])


def _norm_proj(x2d, g, w_bf16):
    t, d = x2d.shape
    n = w_bf16.shape[1]
    full = lambda a: pl.BlockSpec(a.shape, lambda: (0,) * a.ndim)
    return pl.pallas_call(
        _norm_proj_kernel,
        in_specs=[full(x2d), full(g), full(w_bf16)],
        out_specs=pl.BlockSpec((t, n), lambda: (0, 0)),
        out_shape=jax.ShapeDtypeStruct((t, n), _F32),
        compiler_params=pltpu.CompilerParams(vmem_limit_bytes=VMEM_LIMIT),
        name="meta_proj",
    )(x2d, g, w_bf16)


def _meta_state_kernel(p_ref, cw_ref, cb_ref, wg_ref, br_ref, bi_ref, lam_ref, rgn_ref, lb_ref,
                       h_out, tail_out, st_out, xbuf):
    n = N_META
    xbuf[...] = jnp.zeros_like(xbuf)
    h_out[...] = jnp.zeros_like(h_out)
    prm = (cw_ref[...], cb_ref[...], wg_ref[...], br_ref[...], bi_ref[...], lam_ref[...], rgn_ref[...])
    xc, gates = _rg_gates(p_ref[:, 0:D_RG], xbuf, prm)
    _rg_rows(xc, gates, p_ref[:, D_RG:2 * D_RG], h_out, prm)
    tail_out[...] = xbuf[0:SUBLANES, :]

    st_out[...] = jnp.zeros_like(st_out)
    lb = _hg_lower_bound(lb_ref[...])
    pad = HG_CHUNK - n
    valid = lax.broadcasted_iota(jnp.int32, (HG_CHUNK, 1), 0) >= pad
    zeros = jnp.zeros((pad, HG_HEAD_DIM), _F32)
    base = 2 * D_RG
    for hd in range(HG_HEADS):
        lo = hd * HG_HEAD_DIM
        cols = lambda part: slice(base + part * D_HG + lo, base + part * D_HG + lo + HG_HEAD_DIM)
        hq = jnp.concatenate([zeros, p_ref[:, cols(0)]], axis=0)
        hf = jnp.concatenate([zeros, p_ref[:, cols(1)]], axis=0)
        hi = jnp.concatenate([zeros, p_ref[:, cols(2)]], axis=0)
        _hg_chunk(hq, hf, hi, lb[:, lo:lo + HG_HEAD_DIM], st_out, hd, valid=valid)


def _meta_state(p_meta, cw, cb, wg, br, bi, lam, rgn, lb2):
    full = lambda a: pl.BlockSpec(a.shape, lambda: (0,) * a.ndim)
    args = (p_meta, cw, cb, wg, br, bi, lam, rgn, lb2)
    out_shapes = (
        jax.ShapeDtypeStruct((SUBLANES, D_RG), _F32),
        jax.ShapeDtypeStruct((SUBLANES, D_RG), _F32),
        jax.ShapeDtypeStruct((HG_HEADS, HG_HEAD_DIM, HG_HEAD_DIM), _F32),
    )
    return pl.pallas_call(
        _meta_state_kernel,
        in_specs=[full(a) for a in args],
        out_specs=tuple(pl.BlockSpec(s.shape, lambda n=len(s.shape): (0,) * n) for s in out_shapes),
        out_shape=out_shapes,
        scratch_shapes=[pltpu.VMEM((N_META + SUBLANES, D_RG), _F32)],
        name="meta_state",
    )(*args)


def _interleave(streams):
    totals = [sum(c for c, _ in st) for st in streams]
    pos = [0] * len(streams)
    done = [0.0] * len(streams)
    order = []
    while any(p < len(st) for p, st in zip(pos, streams)):
        k = min((i for i in range(len(streams)) if pos[i] < len(streams[i])),
                key=lambda i: done[i] / totals[i])
        cost, thunk = streams[k][pos[k]]
        order.append(thunk)
        pos[k] += 1
        done[k] += cost
    return order


def _tick(xa, xc, o_w, p_w, p_r, y_w, y_r, w, s):
    n = BLOCK_ROWS
    prm, lb, hgn = w["prm"], w["lb"], w["hgn"]
    mxu, vpu = [], []

    def a_norm():
        s["u"][...] = _rms(xa[...], w["mixg"]).astype(_BF16)
    mxu.append((400, a_norm))
    for t in range(D_IN // IN_TILE):
        def a_tile(t=t):
            cols = slice(t * IN_TILE, (t + 1) * IN_TILE)
            p_w[:, cols] = _dot(s["u"][...], w["w_in"][:, cols])
        mxu.append((512, a_tile))

    def c_proj():
        h1 = xc[...] + _dot(y_r[...], w["w_out"][...])
        s["h1"][...] = h1
        s["v"][...] = _rms(h1, w["ffn_g"]).astype(_BF16)
    mxu.append((1400, c_proj))
    for j in range(D_FF // FF_TILE):
        def c_ffn(j=j):
            lo = j * FF_TILE
            v = s["v"][...]
            gate = _dot(v, w["w_gu"][:, lo:lo + FF_TILE])
            up = _dot(v, w["w_gu"][:, D_FF + lo:D_FF + lo + FF_TILE])
            s["act"][:, lo:lo + FF_TILE] = (_silu(gate) * up).astype(_BF16)
        mxu.append((600, c_ffn))
    for t in range(D_MODEL // FF_TILE):
        def c_down(t=t):
            cols = slice(t * FF_TILE, (t + 1) * FF_TILE)
            o_w[:, cols] = s["h1"][:, cols] + _dot(s["act"][...], w["w_down"][:, cols])
        mxu.append((750, c_down))

    def c_norm():
        o_w[...] = _rms(o_w[...], w["final_g"])
    mxu.append((400, c_norm))

    def b_gates():
        xconv, gates = _rg_gates(p_r[:, 0:D_RG], s["xbuf"], prm)
        s["xconv"][...] = xconv
        s["gates"][...] = gates
    vpu.append((900, b_gates))
    for g in range(n // HG_CHUNK):
        def b_rg(g=g):
            rows = slice(g * HG_CHUNK, (g + 1) * HG_CHUNK)
            y = _rg_rows(s["xconv"][rows, :], s["gates"][rows, :], p_r[rows, D_RG:2 * D_RG], s["h"], prm)
            y_w[rows, 0:D_RG] = y.astype(_BF16)
        vpu.append((750, b_rg))
    base = 2 * D_RG
    for ci in range(n // HG_CHUNK):
        for hd in range(HG_HEADS):
            def b_hg(ci=ci, hd=hd):
                rows = slice(ci * HG_CHUNK, (ci + 1) * HG_CHUNK)
                lo = hd * HG_HEAD_DIM
                col = lambda part: slice(base + part * D_HG + lo, base + part * D_HG + lo + HG_HEAD_DIM)
                o = _hg_chunk(p_r[rows, col(0)], p_r[rows, col(1)], p_r[rows, col(2)],
                              lb[:, lo:lo + HG_HEAD_DIM], s["st"], hd)
                o = _rms(o, hgn) * _silu(p_r[rows, col(3)])
                y_w[rows, D_RG + lo:D_RG + lo + HG_HEAD_DIM] = o.astype(_BF16)
            vpu.append((300, b_hg))

    for thunk in _interleave([mxu, vpu]):
        thunk()


def _layer_kernel(blocks_per_seq,
                  xa_ref, xc_ref, mixg_ref, win_ref, h0_ref, tail0_ref, st0_ref,
                  cw_ref, cb_ref, wg_ref, br_ref, bi_ref, lam_ref, rgn_ref, lb_ref, hgn_ref,
                  wo_ref, fg_ref, wgu_ref, wd_ref, ng_ref,
                  o_ref, p0, p1, y0, y1, act, u_s, h1_s, v_s, gates_s, xconv_s, xbuf, h_s, st_s):
    j = pl.program_id(0)
    n = BLOCK_ROWS

    @pl.when(j == 0)
    def _():
        p1[...] = jnp.zeros_like(p1)
        y0[...] = jnp.zeros_like(y0)
        y1[...] = jnp.zeros_like(y1)
        xbuf[0:SUBLANES, :] = jnp.zeros((SUBLANES, D_RG), _F32)
        h_s[...] = jnp.zeros_like(h_s)
        st_s[...] = jnp.zeros_like(st_s)

    w = dict(prm=(cw_ref[...], cb_ref[...], wg_ref[...], br_ref[...], bi_ref[...], lam_ref[...], rgn_ref[...]),
             lb=_hg_lower_bound(lb_ref[...]), hgn=hgn_ref[...], mixg=mixg_ref[...], w_in=win_ref,
             w_out=wo_ref, ffn_g=fg_ref[...], w_gu=wgu_ref, w_down=wd_ref, final_g=ng_ref[...])
    s = dict(u=u_s, h1=h1_s, v=v_s, act=act, gates=gates_s, xconv=xconv_s, xbuf=xbuf, h=h_s, st=st_s)
    lo, hi = pl.ds(0, n), pl.ds(n, n)

    _tick(xa_ref.at[lo], xc_ref.at[lo], o_ref.at[lo], p0, p1, y1, y0, w, s)

    @pl.when(lax.rem(2 * j, blocks_per_seq) == 0)
    def _():
        h_s[...] = h0_ref[...]
        xbuf[0:SUBLANES, :] = tail0_ref[...]
        st_s[...] = st0_ref[...]

    _tick(xa_ref.at[hi], xc_ref.at[hi], o_ref.at[hi], p1, p0, y0, y1, w, s)


def _layer(x2d, seq, mixg, w_in, h0, tail0, st0, mix_prm, hgn, wo, fg, wgu, wd, ng):
    t, d = x2d.shape
    n = BLOCK_ROWS
    assert t % (2 * n) == 0 and seq % (2 * n) == 0
    steps = t // (2 * n)
    const = lambda a: pl.BlockSpec(a.shape, lambda j, nd=a.ndim: (0,) * nd, pipeline_mode=pl.Buffered(1))
    small = (h0, tail0, st0) + tuple(mix_prm) + (hgn,)
    in_specs = ([pl.BlockSpec((2 * n, d), lambda j: (jnp.minimum(j, steps - 1), 0)),
                 pl.BlockSpec((2 * n, d), lambda j: (jnp.maximum(j - 1, 0), 0)),
                 const(mixg), const(w_in)] + [const(a) for a in small]
                + [const(wo), const(fg), const(wgu), const(wd), const(ng)])
    return pl.pallas_call(
        functools.partial(_layer_kernel, seq // n),
        grid=(steps + 1,),
        in_specs=in_specs,
        out_specs=pl.BlockSpec((2 * n, d), lambda j: (jnp.maximum(j - 1, 0), 0)),
        out_shape=jax.ShapeDtypeStruct((t, d), _F32),
        scratch_shapes=[
            pltpu.VMEM((n, D_IN), _F32), pltpu.VMEM((n, D_IN), _F32),
            pltpu.VMEM((n, D_MODEL), _BF16), pltpu.VMEM((n, D_MODEL), _BF16),
            pltpu.VMEM((n, D_FF), _BF16),
            pltpu.VMEM((n, D_MODEL), _BF16),
            pltpu.VMEM((n, D_MODEL), _F32),
            pltpu.VMEM((n, D_MODEL), _BF16),
            pltpu.VMEM((n, 2 * D_RG), _F32),
            pltpu.VMEM((n, D_RG), _F32),
            pltpu.VMEM((n + SUBLANES, D_RG), _F32),
            pltpu.VMEM((SUBLANES, D_RG), _F32),
            pltpu.VMEM((HG_HEADS, HG_HEAD_DIM, HG_HEAD_DIM), _F32),
        ],
        compiler_params=pltpu.CompilerParams(
            dimension_semantics=("arbitrary",), vmem_limit_bytes=VMEM_LIMIT),
        name="layer",
    )(x2d, x2d, mixg, w_in, *small, wo, fg, wgu, wd, ng)


def _block_diag(w):
    h, d, _ = w.shape
    eye = jnp.eye(h, dtype=w.dtype)
    return (eye[:, None, :, None] * w[:, :, None, :]).reshape(h * d, h * d)


def kernel(x, meta_tokens, mix_norm_g, w_in, conv_w, conv_b, w_rgate, b_rgate, w_igate, b_igate,
           lru_lambda, rg_norm_g, hg_lower_bound, hg_norm_g, w_out, ffn_norm_g, w_gate_up, w_down,
           final_norm_g):
    b, s, d = x.shape
    row = lambda a: a.reshape(1, -1).astype(_F32)
    w_in_b = w_in[0].astype(_BF16)
    wg = jnp.concatenate([_block_diag(w_rgate[0]), _block_diag(w_igate[0])], axis=1).astype(_BF16)
    mixg = row(mix_norm_g[0])
    mix_prm = (conv_w[0].astype(_F32), row(conv_b[0]), wg, row(b_rgate[0]), row(b_igate[0]),
               row(lru_lambda[0]), row(rg_norm_g[0]), hg_lower_bound.astype(_F32))

    p_meta = _norm_proj(meta_tokens.astype(x.dtype), mixg, w_in_b)
    h0, tail0, st0 = _meta_state(p_meta, *mix_prm)
    out = _layer(x.reshape(b * s, d), s, mixg, w_in_b, h0, tail0, st0, mix_prm, row(hg_norm_g[0]),
                 w_out[0].astype(_BF16), row(ffn_norm_g[0]), w_gate_up[0].astype(_BF16),
                 w_down[0].astype(_BF16), row(final_norm_g))
    return out.reshape(b, s, d)
```

```python
import functools

import jax
import jax.numpy as jnp
from jax import lax
from jax.experimental import pallas as pl
from jax.experimental.pallas import tpu as pltpu

D_MODEL = 1024
N_META = 16
D_RG = 512
CONV_W = 4
LRU_C = 8.0
D_HG = 512
HG_HEAD_DIM = 128
HG_HEADS = 4
D_IN = 2 * D_RG + 4 * D_HG
D_FF = 2816
EPS = 1e-6

HG_CHUNK = 64
SUBLANES = 8
BLOCK_ROWS = 256
IN_TILE = 512
FF_TILE = 256
VMEM_LIMIT = 60 * 1024 * 1024

_F32 = jnp.float32
_BF16 = jnp.bfloat16
_NT = (((1,), (1,)), ((), ()))
_TN = (((0,), (0,)), ((), ()))


def _rms(x, g):
    return x * lax.rsqrt(jnp.mean(x * x, axis=-1, keepdims=True) + EPS) * g


def _sigmoid(x):
    return 0.5 * jnp.tanh(0.5 * x) + 0.5


def _silu(x):
    return x * _sigmoid(x)


def _gelu_tanh(x):
    c = 0.7978845608028654
    t = jnp.tanh(x * (c + (c * 0.044715) * (x * x)))
    return (0.5 * x) * (1.0 + t)


def _dot(a, b):
    return jnp.dot(a, b, preferred_element_type=_F32)


def _slab_roll(x, shift):
    groups = [pltpu.roll(x[i:i + SUBLANES], shift, 0) for i in range(0, x.shape[0], SUBLANES)]
    return jnp.concatenate(groups, axis=0)


def _rg_gates(xr, xbuf, prm):
    conv_w, conv_b, w_gate = prm[0], prm[1], prm[2]
    n = xr.shape[0]
    xbuf[SUBLANES:SUBLANES + n, :] = xr
    xc = conv_b + conv_w[3:4, :] * xr
    for j in range(1, CONV_W):
        xc = xc + conv_w[CONV_W - 1 - j:CONV_W - j, :] * xbuf[SUBLANES - j:SUBLANES - j + n, :]
    xbuf[0:SUBLANES, :] = xr[n - SUBLANES:n, :]
    return xc, _dot(xc.astype(_BF16), w_gate)


def _rg_rows(xc, gates, gr, h_ref, prm):
    b_r, b_i, lam, norm_g = prm[3], prm[4], prm[5], prm[6]
    n = xc.shape[0]
    r = _sigmoid(gates[:, :D_RG] + b_r)
    i = _sigmoid(gates[:, D_RG:] + b_i)
    z = -lam
    softplus = jnp.maximum(z, 0.0) + jnp.log1p(jnp.exp(-jnp.abs(z)))
    log_a = (-LRU_C * softplus) * r
    a = jnp.exp(log_a)
    bx = jnp.sqrt(-jnp.tanh(log_a) * (a * a + 1.0)) * (i * xc)

    pos = lax.broadcasted_iota(jnp.int32, (n, 1), 0) & (SUBLANES - 1)
    for s in (1, 2, 4):
        keep = pos >= s
        a_prev = _slab_roll(a, s)
        b_prev = _slab_roll(bx, s)
        bx = jnp.where(keep, a * b_prev + bx, bx)
        a = jnp.where(keep, a * a_prev, a)
    h = h_ref[...]
    hs = []
    for g in range(n // SUBLANES):
        rows = slice(g * SUBLANES, (g + 1) * SUBLANES)
        hg = a[rows] * h + bx[rows]
        hs.append(hg)
        h = jnp.broadcast_to(hg[SUBLANES - 1:SUBLANES, :], (SUBLANES, D_RG))
    h_ref[...] = h
    y = _gelu_tanh(gr) * jnp.concatenate(hs, axis=0)
    return _rms(y, norm_g)


def _hg_stages(hq, hf, hi, lb, st_ref, head, finish, valid=None):
    c = HG_CHUNK
    q = _silu(hq)
    f = lb + (1.0 - lb) * _sigmoid(hf)
    v = hi
    if valid is not None:
        q = jnp.where(valid, q, 0.0)
        f = jnp.where(valid, f, 1.0)
        v = jnp.where(valid, v, 0.0)
    k = 1.0 - f
    vb = v.astype(_BF16)

    row = lax.broadcasted_iota(jnp.int32, (c, 1), 0)
    pos = row & (SUBLANES - 1)

    o_band = jnp.sum(q * k, axis=-1, keepdims=True) * v
    qw = q
    for d in range(1, SUBLANES):
        qw = qw * (f if d == 1 else _slab_roll(f, d - 1))
        a = jnp.sum(qw * _slab_roll(k, d), axis=-1, keepdims=True)
        o_band = o_band + jnp.where(pos >= d, a, 0.0) * _slab_roll(v, d)

    tot = f
    pre = f
    suf = jnp.ones_like(f)
    m = 1
    while m < SUBLANES:
        upper = (row & m) != 0
        sib = jnp.where(upper, _slab_roll(tot, m), _slab_roll(tot, SUBLANES - m))
        pre = jnp.where(upper, pre * sib, pre)
        suf = jnp.where(upper, suf, suf * sib)
        tot = tot * sib
        m *= 2

    ns = c // SUBLANES
    slabs = lambda a: [a[i * SUBLANES:(i + 1) * SUBLANES] for i in range(ns)]
    q_s, k_s, pre_s, suf_s, tot_s = slabs(q), slabs(k), slabs(pre), slabs(suf), slabs(tot)
    zero = jnp.zeros((SUBLANES, q.shape[1]), _F32)
    operands = []
    g = 1
    while g < ns:
        lhs = jnp.concatenate([q_s[i] * pre_s[i] if i & g else zero for i in range(ns)], axis=0)
        rhs = jnp.concatenate([zero if i & g else k_s[i] * suf_s[i] for i in range(ns)], axis=0)
        operands.append((g * SUBLANES, lhs.astype(_BF16), rhs.astype(_BF16)))
        sib = [tot_s[i ^ g] for i in range(ns)]
        pre_s = [pre_s[i] * sib[i] if i & g else pre_s[i] for i in range(ns)]
        suf_s = [suf_s[i] if i & g else suf_s[i] * sib[i] for i in range(ns)]
        tot_s = [tot_s[i] * sib[i] for i in range(ns)]
        g *= 2
    qp = (q * jnp.concatenate(pre_s, axis=0)).astype(_BF16)
    ks = (k * jnp.concatenate(suf_s, axis=0)).astype(_BF16)
    tot = tot_s[0]
    yield

    st = st_ref[head]
    prods = [(m, lax.dot_general(lhs, rhs, _NT, preferred_element_type=_F32)) for m, lhs, rhs in operands]
    inter = lax.dot_general(qp, st.astype(_BF16), _NT, preferred_element_type=_F32)
    upd = lax.dot_general(vb, ks, _TN, preferred_element_type=_F32)
    yield

    ti = lax.broadcasted_iota(jnp.int32, (c, c), 0)
    si = lax.broadcasted_iota(jnp.int32, (c, c), 1)
    att = None
    for m, prod in prods:
        if 2 * m < c:
            shift = (2 * m).bit_length() - 1
            prod = jnp.where((ti >> shift) == (si >> shift), prod, 0.0)
        att = prod if att is None else att + prod
    intra = _dot(att.astype(_BF16), vb)
    st_ref[head] = st * tot[0:1, :] + upd
    yield

    finish(inter + intra + o_band)


def _hg_chunk(hq, hf, hi, lb, st_ref, head, valid=None):
    out = []
    for _ in _hg_stages(hq, hf, hi, lb, st_ref, head, out.append, valid=valid):
        pass
    return out[0]


def _hg_lower_bound(lb2):
    m = jnp.maximum(lb2[0:1, :], lb2[1:2, :])
    e0 = jnp.exp(lb2[0:1, :] - m)
    e1 = jnp.exp(lb2[1:2, :] - m)
    return e0 / (e0 + e1)


def _norm_proj_kernel(x_ref, g_ref, w_ref, o_ref):
    u = _rms(x_ref[...], g_ref[...]).astype(_BF16)
    o_ref[...] = _dot(u, w_ref[...])


def _norm_proj(x2d, g, w_bf16):
    t, d = x2d.shape
    n = w_bf16.shape[1]
    full = lambda a: pl.BlockSpec(a.shape, lambda: (0,) * a.ndim)
    return pl.pallas_call(
        _norm_proj_kernel,
        in_specs=[full(x2d), full(g), full(w_bf16)],
        out_specs=pl.BlockSpec((t, n), lambda: (0, 0)),
        out_shape=jax.ShapeDtypeStruct((t, n), _F32),
        compiler_params=pltpu.CompilerParams(vmem_limit_bytes=VMEM_LIMIT),
        name="meta_proj",
    )(x2d, g, w_bf16)


def _meta_state_kernel(p_ref, cw_ref, cb_ref, wg_ref, br_ref, bi_ref, lam_ref, rgn_ref, lb_ref,
                       h_out, tail_out, st_out, xbuf):
    n = N_META
    xbuf[...] = jnp.zeros_like(xbuf)
    h_out[...] = jnp.zeros_like(h_out)
    prm = (cw_ref[...], cb_ref[...], wg_ref[...], br_ref[...], bi_ref[...], lam_ref[...], rgn_ref[...])
    xc, gates = _rg_gates(p_ref[:, 0:D_RG], xbuf, prm)
    _rg_rows(xc, gates, p_ref[:, D_RG:2 * D_RG], h_out, prm)
    tail_out[...] = xbuf[0:SUBLANES, :]

    st_out[...] = jnp.zeros_like(st_out)
    lb = _hg_lower_bound(lb_ref[...])
    pad = HG_CHUNK - n
    valid = lax.broadcasted_iota(jnp.int32, (HG_CHUNK, 1), 0) >= pad
    zeros = jnp.zeros((pad, HG_HEAD_DIM), _F32)
    base = 2 * D_RG
    for hd in range(HG_HEADS):
        lo = hd * HG_HEAD_DIM
        cols = lambda part: slice(base + part * D_HG + lo, base + part * D_HG + lo + HG_HEAD_DIM)
        hq = jnp.concatenate([zeros, p_ref[:, cols(0)]], axis=0)
        hf = jnp.concatenate([zeros, p_ref[:, cols(1)]], axis=0)
        hi = jnp.concatenate([zeros, p_ref[:, cols(2)]], axis=0)
        _hg_chunk(hq, hf, hi, lb[:, lo:lo + HG_HEAD_DIM], st_out, hd, valid=valid)


def _meta_state(p_meta, cw, cb, wg, br, bi, lam, rgn, lb2):
    full = lambda a: pl.BlockSpec(a.shape, lambda: (0,) * a.ndim)
    args = (p_meta, cw, cb, wg, br, bi, lam, rgn, lb2)
    out_shapes = (
        jax.ShapeDtypeStruct((SUBLANES, D_RG), _F32),
        jax.ShapeDtypeStruct((SUBLANES, D_RG), _F32),
        jax.ShapeDtypeStruct((HG_HEADS, HG_HEAD_DIM, HG_HEAD_DIM), _F32),
    )
    return pl.pallas_call(
        _meta_state_kernel,
        in_specs=[full(a) for a in args],
        out_specs=tuple(pl.BlockSpec(s.shape, lambda n=len(s.shape): (0,) * n) for s in out_shapes),
        out_shape=out_shapes,
        scratch_shapes=[pltpu.VMEM((N_META + SUBLANES, D_RG), _F32)],
        name="meta_state",
    )(*args)


def _interleave(streams):
    totals = [sum(c for c, _ in st) for st in streams]
    pos = [0] * len(streams)
    done = [0.0] * len(streams)
    order = []
    while any(p < len(st) for p, st in zip(pos, streams)):
        k = min((i for i in range(len(streams)) if pos[i] < len(streams[i])),
                key=lambda i: done[i] / totals[i])
        cost, thunk = streams[k][pos[k]]
        order.append(thunk)
        pos[k] += 1
        done[k] += cost
    return order


def _tick(xa, xc, o_w, p_w, p_r, y_w, y_r, w, s):
    n = BLOCK_ROWS
    prm, lb, hgn = w["prm"], w["lb"], w["hgn"]
    mxu, vpu = [], []

    def a_norm():
        s["u"][...] = _rms(xa[...], w["mixg"]).astype(_BF16)
    a_tiles = []
    for t in range(D_IN // IN_TILE):
        def a_tile(t=t):
            cols = slice(t * IN_TILE, (t + 1) * IN_TILE)
            p_w[:, cols] = _dot(s["u"][...], w["w_in"][:, cols])
        a_tiles.append((512, a_tile))

    mxu.append((400, a_norm))
    mxu.extend(a_tiles)
    for t in range(D_MODEL // FF_TILE):
        def c_proj(t=t):
            cols = slice(t * FF_TILE, (t + 1) * FF_TILE)
            s["h1"][:, cols] = xc[:, cols] + _dot(y_r[...], w["w_out"][:, cols])
        mxu.append((300, c_proj))

    def c_rms():
        s["v"][...] = _rms(s["h1"][...], w["ffn_g"]).astype(_BF16)
    mxu.append((250, c_rms))
    pending = {}
    n_ff = D_FF // FF_TILE
    for j in range(n_ff + 1):
        def c_gate_up(j=j):
            lo = j * FF_TILE
            v = s["v"][...]
            pending[j] = (_dot(v, w["w_gu"][:, lo:lo + FF_TILE]),
                          _dot(v, w["w_gu"][:, D_FF + lo:D_FF + lo + FF_TILE]))

        def c_act(j=j - 1):
            gate, up = pending.pop(j)
            s["act"][:, j * FF_TILE:(j + 1) * FF_TILE] = (_silu(gate) * up).astype(_BF16)
        if j < n_ff:
            mxu.append((512, c_gate_up))
        if j > 0:
            mxu.append((100, c_act))
    for t in range(D_MODEL // FF_TILE):
        def c_down(t=t):
            cols = slice(t * FF_TILE, (t + 1) * FF_TILE)
            o_w[:, cols] = s["h1"][:, cols] + _dot(s["act"][...], w["w_down"][:, cols])
        mxu.append((750, c_down))

    def c_norm():
        o_w[...] = _rms(o_w[...], w["final_g"])
    mxu.append((400, c_norm))

    def b_gates():
        xconv, gates = _rg_gates(p_r[:, 0:D_RG], s["xbuf"], prm)
        s["xconv"][...] = xconv
        s["gates"][...] = gates
    vpu.append((900, b_gates))
    for g in range(n // HG_CHUNK):
        def b_rg(g=g):
            rows = slice(g * HG_CHUNK, (g + 1) * HG_CHUNK)
            y = _rg_rows(s["xconv"][rows, :], s["gates"][rows, :], p_r[rows, D_RG:2 * D_RG], s["h"], prm)
            y_w[rows, 0:D_RG] = y.astype(_BF16)
        vpu.append((750, b_rg))
    base = 2 * D_RG

    def hg_piece(ci, hd):
        rows = slice(ci * HG_CHUNK, (ci + 1) * HG_CHUNK)
        lo = hd * HG_HEAD_DIM
        col = lambda part: slice(base + part * D_HG + lo, base + part * D_HG + lo + HG_HEAD_DIM)

        def finish(o):
            o = _rms(o, hgn) * _silu(p_r[rows, col(3)])
            y_w[rows, D_RG + lo:D_RG + lo + HG_HEAD_DIM] = o.astype(_BF16)
        return _hg_stages(p_r[rows, col(0)], p_r[rows, col(1)], p_r[rows, col(2)],
                          lb[:, lo:lo + HG_HEAD_DIM], s["st"], hd, finish)

    pieces = [(ci, hd) for ci in range(n // HG_CHUNK) for hd in range(HG_HEADS)]
    gens = {}
    stage_cost = (260, 40, 40, 40)

    def advance(i):
        if i not in gens:
            gens[i] = hg_piece(*pieces[i])
        next(gens[i], None)

    for r in range(len(pieces) + len(stage_cost) - 1):
        for stage in (1, 2, 3, 0):
            i = r - stage
            if 0 <= i < len(pieces):
                vpu.append((stage_cost[stage], lambda i=i: advance(i)))

    for thunk in _interleave([mxu, vpu]):
        thunk()


def _layer_kernel(blocks_per_seq,
                  xa_ref, xc_ref, mixg_ref, win_ref, h0_ref, tail0_ref, st0_ref,
                  cw_ref, cb_ref, wg_ref, br_ref, bi_ref, lam_ref, rgn_ref, lb_ref, hgn_ref,
                  wo_ref, fg_ref, wgu_ref, wd_ref, ng_ref,
                  o_ref, p0, p1, y0, y1, act, u_s, h1_s, v_s, gates_s, xconv_s, xbuf, h_s, st_s):
    j = pl.program_id(0)
    n = BLOCK_ROWS

    @pl.when(j == 0)
    def _():
        p1[...] = jnp.zeros_like(p1)
        y0[...] = jnp.zeros_like(y0)
        y1[...] = jnp.zeros_like(y1)
        xbuf[0:SUBLANES, :] = jnp.zeros((SUBLANES, D_RG), _F32)
        h_s[...] = jnp.zeros_like(h_s)
        st_s[...] = jnp.zeros_like(st_s)

    w = dict(prm=(cw_ref[...], cb_ref[...], wg_ref[...], br_ref[...], bi_ref[...], lam_ref[...], rgn_ref[...]),
             lb=_hg_lower_bound(lb_ref[...]), hgn=hgn_ref[...], mixg=mixg_ref[...], w_in=win_ref,
             w_out=wo_ref, ffn_g=fg_ref[...], w_gu=wgu_ref, w_down=wd_ref, final_g=ng_ref[...])
    s = dict(u=u_s, h1=h1_s, v=v_s, act=act, gates=gates_s, xconv=xconv_s, xbuf=xbuf, h=h_s, st=st_s)
    lo, hi = pl.ds(0, n), pl.ds(n, n)

    _tick(xa_ref.at[lo], xc_ref.at[lo], o_ref.at[lo], p0, p1, y1, y0, w, s)

    @pl.when(lax.rem(2 * j, blocks_per_seq) == 0)
    def _():
        h_s[...] = h0_ref[...]
        xbuf[0:SUBLANES, :] = tail0_ref[...]
        st_s[...] = st0_ref[...]

    _tick(xa_ref.at[hi], xc_ref.at[hi], o_ref.at[hi], p1, p0, y0, y1, w, s)


def _layer(x2d, seq, mixg, w_in, h0, tail0, st0, mix_prm, hgn, wo, fg, wgu, wd, ng):
    t, d = x2d.shape
    n = BLOCK_ROWS
    assert t % (2 * n) == 0 and seq % (2 * n) == 0
    steps = t // (2 * n)
    const = lambda a: pl.BlockSpec(a.shape, lambda j, nd=a.ndim: (0,) * nd, pipeline_mode=pl.Buffered(1))
    small = (h0, tail0, st0) + tuple(mix_prm) + (hgn,)
    in_specs = ([pl.BlockSpec((2 * n, d), lambda j: (jnp.minimum(j, steps - 1), 0)),
                 pl.BlockSpec((2 * n, d), lambda j: (jnp.maximum(j - 1, 0), 0)),
                 const(mixg), const(w_in)] + [const(a) for a in small]
                + [const(wo), const(fg), const(wgu), const(wd), const(ng)])
    return pl.pallas_call(
        functools.partial(_layer_kernel, seq // n),
        grid=(steps + 1,),
        in_specs=in_specs,
        out_specs=pl.BlockSpec((2 * n, d), lambda j: (jnp.maximum(j - 1, 0), 0)),
        out_shape=jax.ShapeDtypeStruct((t, d), _F32),
        scratch_shapes=[
            pltpu.VMEM((n, D_IN), _F32), pltpu.VMEM((n, D_IN), _F32),
            pltpu.VMEM((n, D_MODEL), _BF16), pltpu.VMEM((n, D_MODEL), _BF16),
            pltpu.VMEM((n, D_FF), _BF16),
            pltpu.VMEM((n, D_MODEL), _BF16),
            pltpu.VMEM((n, D_MODEL), _F32),
            pltpu.VMEM((n, D_MODEL), _BF16),
            pltpu.VMEM((n, 2 * D_RG), _F32),
            pltpu.VMEM((n, D_RG), _F32),
            pltpu.VMEM((n + SUBLANES, D_RG), _F32),
            pltpu.VMEM((SUBLANES, D_RG), _F32),
            pltpu.VMEM((HG_HEADS, HG_HEAD_DIM, HG_HEAD_DIM), _F32),
        ],
        compiler_params=pltpu.CompilerParams(
            dimension_semantics=("arbitrary",), vmem_limit_bytes=VMEM_LIMIT),
        name="layer",
    )(x2d, x2d, mixg, w_in, *small, wo, fg, wgu, wd, ng)


def _block_diag(w):
    h, d, _ = w.shape
    eye = jnp.eye(h, dtype=w.dtype)
    return (eye[:, None, :, None] * w[:, :, None, :]).reshape(h * d, h * d)


def kernel(x, meta_tokens, mix_norm_g, w_in, conv_w, conv_b, w_rgate, b_rgate, w_igate, b_igate,
           lru_lambda, rg_norm_g, hg_lower_bound, hg_norm_g, w_out, ffn_norm_g, w_gate_up, w_down,
           final_norm_g):
    b, s, d = x.shape
    row = lambda a: a.reshape(1, -1).astype(_F32)
    w_in_b = w_in[0].astype(_BF16)
    wg = jnp.concatenate([_block_diag(w_rgate[0]), _block_diag(w_igate[0])], axis=1).astype(_BF16)
    mixg = row(mix_norm_g[0])
    mix_prm = (conv_w[0].astype(_F32), row(conv_b[0]), wg, row(b_rgate[0]), row(b_igate[0]),
               row(lru_lambda[0]), row(rg_norm_g[0]), hg_lower_bound.astype(_F32))

    p_meta = _norm_proj(meta_tokens.astype(x.dtype), mixg, w_in_b)
    h0, tail0, st0 = _meta_state(p_meta, *mix_prm)
    out = _layer(x.reshape(b * s, d), s, mixg, w_in_b, h0, tail0, st0, mix_prm, row(hg_norm_g[0]),
                 w_out[0].astype(_BF16), row(ffn_norm_g[0]), w_gate_up[0].astype(_BF16),
                 w_down[0].astype(_BF16), row(final_norm_g))
    return out.reshape(b, s, d)
```

```python
import functools

import jax
import jax.numpy as jnp
from jax import lax
from jax.experimental import pallas as pl
from jax.experimental.pallas import tpu as pltpu

D_MODEL = 1024
N_META = 16
D_RG = 512
CONV_W = 4
LRU_C = 8.0
D_HG = 512
HG_HEAD_DIM = 128
HG_HEADS = 4
D_IN = 2 * D_RG + 4 * D_HG
D_FF = 2816
EPS = 1e-6

GATE_PAIR = 128
HG_CHUNK = 64
SUBLANES = 8
BLOCK_ROWS = 256
IN_TILE = 512
FF_TILE = 256
VMEM_LIMIT = 60 * 1024 * 1024

_F32 = jnp.float32
_BF16 = jnp.bfloat16
_NT = (((1,), (1,)), ((), ()))
_TN = (((0,), (0,)), ((), ()))


def _rms(x, g):
    return x * lax.rsqrt(jnp.mean(x * x, axis=-1, keepdims=True) + EPS) * g


def _silu(x):
    hx = 0.5 * x
    return hx + hx * jnp.tanh(hx)


def _scaled_sigmoid(x, scale):
    hs = 0.5 * scale
    return hs + hs * jnp.tanh(0.5 * x)


def _gelu_tanh(x):
    c = 0.7978845608028654
    t = jnp.tanh(x * (c + (c * 0.044715) * (x * x)))
    return (0.5 * x) * (1.0 + t)


def _dot(a, b):
    return lax.dot_general(a, b, (((1,), (0,)), ((), ())), preferred_element_type=_F32)


def _slab_roll(x, shift):
    groups = [pltpu.roll(x[i:i + SUBLANES], shift, 0) for i in range(0, x.shape[0], SUBLANES)]
    return jnp.concatenate(groups, axis=0)


def _rg_gates(xr, xbuf, prm):
    conv_w, conv_b, w_gate = prm[0], prm[1], prm[2]
    n = xr.shape[0]
    xbuf[SUBLANES:SUBLANES + n, :] = xr
    xc = conv_b + conv_w[3:4, :] * xr
    for j in range(1, CONV_W):
        xc = xc + conv_w[CONV_W - 1 - j:CONV_W - j, :] * xbuf[SUBLANES - j:SUBLANES - j + n, :]
    xbuf[0:SUBLANES, :] = xr[n - SUBLANES:n, :]
    xb = xc.astype(_BF16)
    parts = [_dot(xb[:, p * GATE_PAIR:(p + 1) * GATE_PAIR], w_gate[p]) for p in range(D_RG // GATE_PAIR)]
    gates = jnp.concatenate([part[:, :GATE_PAIR] for part in parts] + [part[:, GATE_PAIR:] for part in parts],
                            axis=1)
    return xc, gates


def _rg_rows(xc, gates, gr, h_ref, prm):
    b_r, b_i, lam, norm_g = prm[3], prm[4], prm[5], prm[6]
    n = xc.shape[0]
    z = -lam
    softplus = jnp.maximum(z, 0.0) + jnp.log1p(jnp.exp(-jnp.abs(z)))
    log_a = _scaled_sigmoid(gates[:, :D_RG] + b_r, -LRU_C * softplus)
    a = jnp.exp(log_a)
    ix = _scaled_sigmoid(gates[:, D_RG:] + b_i, xc)
    bx = jnp.sqrt(-jnp.tanh(log_a) * (a * a + 1.0)) * ix

    pos = lax.broadcasted_iota(jnp.int32, (n, 1), 0) & (SUBLANES - 1)
    for s in (1, 2, 4):
        keep = pos >= s
        a_prev = _slab_roll(a, s)
        b_prev = _slab_roll(bx, s)
        bx = jnp.where(keep, a * b_prev + bx, bx)
        a = jnp.where(keep, a * a_prev, a)
    h = h_ref[...]
    hs = []
    for g in range(n // SUBLANES):
        rows = slice(g * SUBLANES, (g + 1) * SUBLANES)
        hg = a[rows] * h + bx[rows]
        hs.append(hg)
        h = jnp.broadcast_to(hg[SUBLANES - 1:SUBLANES, :], (SUBLANES, D_RG))
    h_ref[...] = h
    y = _gelu_tanh(gr) * jnp.concatenate(hs, axis=0)
    return _rms(y, norm_g)


def _hg_stages(hq, hf, hi, lb, st_ref, head, finish, valid=None):
    c = HG_CHUNK
    q = _silu(hq)
    half = 0.5 * (1.0 - lb)
    f = (lb + half) + half * jnp.tanh(0.5 * hf)
    v = hi
    if valid is not None:
        q = jnp.where(valid, q, 0.0)
        f = jnp.where(valid, f, 1.0)
        v = jnp.where(valid, v, 0.0)
    k = 1.0 - f
    vb = v.astype(_BF16)

    row = lax.broadcasted_iota(jnp.int32, (c, 1), 0)
    pos = row & (SUBLANES - 1)

    kd = k
    vd = v
    o_band = jnp.sum(q * kd, axis=-1, keepdims=True) * vd
    for d in range(1, SUBLANES):
        kd = _slab_roll(kd, 1) * f
        vd = _slab_roll(vd, 1)
        a = jnp.sum(q * kd, axis=-1, keepdims=True)
        o_band = o_band + jnp.where(pos >= d, a, 0.0) * vd

    tot = f
    pre = f
    suf = jnp.ones_like(f)
    m = 1
    while m < SUBLANES:
        upper = (row & m) != 0
        sib = jnp.where(upper, _slab_roll(tot, m), _slab_roll(tot, SUBLANES - m))
        pre = jnp.where(upper, pre * sib, pre)
        suf = jnp.where(upper, suf, suf * sib)
        tot = tot * sib
        m *= 2

    ns = c // SUBLANES
    slabs = lambda a: [a[i * SUBLANES:(i + 1) * SUBLANES] for i in range(ns)]
    q_s, k_s, pre_s, suf_s, tot_s = slabs(q), slabs(k), slabs(pre), slabs(suf), slabs(tot)
    zero = jnp.zeros((SUBLANES, q.shape[1]), _F32)
    operands = []
    g = 1
    while g < ns:
        lhs = jnp.concatenate([q_s[i] * pre_s[i] if i & g else zero for i in range(ns)], axis=0)
        rhs = jnp.concatenate([zero if i & g else k_s[i] * suf_s[i] for i in range(ns)], axis=0)
        operands.append((g * SUBLANES, lhs.astype(_BF16), rhs.astype(_BF16)))
        sib = [tot_s[i ^ g] for i in range(ns)]
        pre_s = [pre_s[i] * sib[i] if i & g else pre_s[i] for i in range(ns)]
        suf_s = [suf_s[i] if i & g else suf_s[i] * sib[i] for i in range(ns)]
        tot_s = [tot_s[i] * sib[i] for i in range(ns)]
        g *= 2
    qp = (q * jnp.concatenate(pre_s, axis=0)).astype(_BF16)
    ks = (k * jnp.concatenate(suf_s, axis=0)).astype(_BF16)
    tot = tot_s[0]
    yield

    st = st_ref[head]
    prods = [(m, lax.dot_general(lhs, rhs, _NT, preferred_element_type=_F32)) for m, lhs, rhs in operands]
    inter = lax.dot_general(qp, st.astype(_BF16), _NT, preferred_element_type=_F32)
    upd = lax.dot_general(vb, ks, _TN, preferred_element_type=_F32)
    yield

    ti = lax.broadcasted_iota(jnp.int32, (c, c), 0)
    si = lax.broadcasted_iota(jnp.int32, (c, c), 1)
    att = None
    for m, prod in prods:
        if 2 * m < c:
            shift = (2 * m).bit_length() - 1
            prod = jnp.where((ti >> shift) == (si >> shift), prod, 0.0)
        att = prod if att is None else att + prod
    intra = _dot(att.astype(_BF16), vb)
    st_ref[head] = st * tot[0:1, :] + upd
    yield

    finish(inter + intra + o_band)


def _hg_chunk(hq, hf, hi, lb, st_ref, head, valid=None):
    out = []
    for _ in _hg_stages(hq, hf, hi, lb, st_ref, head, out.append, valid=valid):
        pass
    return out[0]


def _hg_lower_bound(lb2):
    m = jnp.maximum(lb2[0:1, :], lb2[1:2, :])
    e0 = jnp.exp(lb2[0:1, :] - m)
    e1 = jnp.exp(lb2[1:2, :] - m)
    return e0 / (e0 + e1)


def _norm_proj_kernel(x_ref, g_ref, w_ref, o_ref):
    u = _rms(x_ref[...], g_ref[...]).astype(_BF16)
    o_ref[...] = _dot(u, w_ref[...])


def _norm_proj(x2d, g, w_bf16):
    t, d = x2d.shape
    n = w_bf16.shape[1]
    full = lambda a: pl.BlockSpec(a.shape, lambda: (0,) * a.ndim)
    return pl.pallas_call(
        _norm_proj_kernel,
        in_specs=[full(x2d), full(g), full(w_bf16)],
        out_specs=pl.BlockSpec((t, n), lambda: (0, 0)),
        out_shape=jax.ShapeDtypeStruct((t, n), _F32),
        compiler_params=pltpu.CompilerParams(vmem_limit_bytes=VMEM_LIMIT),
        name="meta_proj",
    )(x2d, g, w_bf16)


def _meta_state_kernel(p_ref, cw_ref, cb_ref, wg_ref, br_ref, bi_ref, lam_ref, rgn_ref, lb_ref,
                       h_out, tail_out, st_out, xbuf):
    n = N_META
    xbuf[...] = jnp.zeros_like(xbuf)
    h_out[...] = jnp.zeros_like(h_out)
    prm = (cw_ref[...], cb_ref[...], wg_ref[...], br_ref[...], bi_ref[...], lam_ref[...], rgn_ref[...])
    xc, gates = _rg_gates(p_ref[:, 0:D_RG], xbuf, prm)
    _rg_rows(xc, gates, p_ref[:, D_RG:2 * D_RG], h_out, prm)
    tail_out[...] = xbuf[0:SUBLANES, :]

    st_out[...] = jnp.zeros_like(st_out)
    lb = _hg_lower_bound(lb_ref[...])
    pad = HG_CHUNK - n
    valid = lax.broadcasted_iota(jnp.int32, (HG_CHUNK, 1), 0) >= pad
    zeros = jnp.zeros((pad, HG_HEAD_DIM), _F32)
    base = 2 * D_RG
    for hd in range(HG_HEADS):
        lo = hd * HG_HEAD_DIM
        cols = lambda part: slice(base + part * D_HG + lo, base + part * D_HG + lo + HG_HEAD_DIM)
        hq = jnp.concatenate([zeros, p_ref[:, cols(0)]], axis=0)
        hf = jnp.concatenate([zeros, p_ref[:, cols(1)]], axis=0)
        hi = jnp.concatenate([zeros, p_ref[:, cols(2)]], axis=0)
        _hg_chunk(hq, hf, hi, lb[:, lo:lo + HG_HEAD_DIM], st_out, hd, valid=valid)


def _meta_state(p_meta, cw, cb, wg, br, bi, lam, rgn, lb2):
    full = lambda a: pl.BlockSpec(a.shape, lambda: (0,) * a.ndim)
    args = (p_meta, cw, cb, wg, br, bi, lam, rgn, lb2)
    out_shapes = (
        jax.ShapeDtypeStruct((SUBLANES, D_RG), _F32),
        jax.ShapeDtypeStruct((SUBLANES, D_RG), _F32),
        jax.ShapeDtypeStruct((HG_HEADS, HG_HEAD_DIM, HG_HEAD_DIM), _F32),
    )
    return pl.pallas_call(
        _meta_state_kernel,
        in_specs=[full(a) for a in args],
        out_specs=tuple(pl.BlockSpec(s.shape, lambda n=len(s.shape): (0,) * n) for s in out_shapes),
        out_shape=out_shapes,
        scratch_shapes=[pltpu.VMEM((N_META + SUBLANES, D_RG), _F32)],
        name="meta_state",
    )(*args)


def _interleave(streams):
    totals = [sum(c for c, _ in st) for st in streams]
    pos = [0] * len(streams)
    done = [0.0] * len(streams)
    order = []
    while any(p < len(st) for p, st in zip(pos, streams)):
        k = min((i for i in range(len(streams)) if pos[i] < len(streams[i])),
                key=lambda i: done[i] / totals[i])
        cost, thunk = streams[k][pos[k]]
        order.append(thunk)
        pos[k] += 1
        done[k] += cost
    return order


def _tick(xa, xc, o_w, p_w, p_r, y_w, y_r, w, s):
    n = BLOCK_ROWS
    prm, lb, hgn = w["prm"], w["lb"], w["hgn"]
    mxu, vpu = [], []

    val = {}

    def a_norm():
        val["u"] = _rms(xa[...], w["mixg"]).astype(_BF16)
    mxu.append((400, a_norm))
    for t in range(D_IN // IN_TILE):
        def a_tile(t=t):
            cols = slice(t * IN_TILE, (t + 1) * IN_TILE)
            p_w[:, cols] = _dot(val["u"], w["w_in"][:, cols])
        mxu.append((512, a_tile))

    for t in range(D_MODEL // FF_TILE):
        def c_proj(t=t):
            if "y" not in val:
                val["y"] = y_r[...]
            cols = slice(t * FF_TILE, (t + 1) * FF_TILE)
            s["h1"][:, cols] = xc[:, cols] + _dot(val["y"], w["w_out"][:, cols])
        mxu.append((300, c_proj))

    def c_rms():
        val["v"] = _rms(s["h1"][...], w["ffn_g"]).astype(_BF16)
    mxu.append((250, c_rms))
    pending = {}
    n_ff = D_FF // FF_TILE
    for j in range(n_ff + 1):
        def c_gate_up(j=j):
            lo = j * FF_TILE
            pending[j] = (_dot(val["v"], w["w_gu"][:, lo:lo + FF_TILE]),
                          _dot(val["v"], w["w_gu"][:, D_FF + lo:D_FF + lo + FF_TILE]))

        def c_act(j=j - 1):
            gate, up = pending.pop(j)
            s["act"][:, j * FF_TILE:(j + 1) * FF_TILE] = (_silu(gate) * up).astype(_BF16)
        if j < n_ff:
            mxu.append((512, c_gate_up))
        if j > 0:
            mxu.append((100, c_act))
    for t in range(D_MODEL // FF_TILE):
        def c_down(t=t):
            if "act" not in val:
                val["act"] = s["act"][...]
            cols = slice(t * FF_TILE, (t + 1) * FF_TILE)
            o_w[:, cols] = s["h1"][:, cols] + _dot(val["act"], w["w_down"][:, cols])
        mxu.append((750, c_down))

    def c_norm():
        o_w[...] = _rms(o_w[...], w["final_g"])
    mxu.append((400, c_norm))

    def b_gates():
        xconv, gates = _rg_gates(p_r[:, 0:D_RG], s["xbuf"], prm)
        s["xconv"][...] = xconv
        s["gates"][...] = gates
    vpu.append((900, b_gates))
    for g in range(n // HG_CHUNK):
        def b_rg(g=g):
            rows = slice(g * HG_CHUNK, (g + 1) * HG_CHUNK)
            y = _rg_rows(s["xconv"][rows, :], s["gates"][rows, :], p_r[rows, D_RG:2 * D_RG], s["h"], prm)
            y_w[rows, 0:D_RG] = y.astype(_BF16)
        vpu.append((750, b_rg))
    base = 2 * D_RG

    def hg_piece(ci, hd):
        rows = slice(ci * HG_CHUNK, (ci + 1) * HG_CHUNK)
        lo = hd * HG_HEAD_DIM
        col = lambda part: slice(base + part * D_HG + lo, base + part * D_HG + lo + HG_HEAD_DIM)

        def finish(o):
            o = _rms(o, hgn) * _silu(p_r[rows, col(3)])
            y_w[rows, D_RG + lo:D_RG + lo + HG_HEAD_DIM] = o.astype(_BF16)
        return _hg_stages(p_r[rows, col(0)], p_r[rows, col(1)], p_r[rows, col(2)],
                          lb[:, lo:lo + HG_HEAD_DIM], s["st"], hd, finish)

    pieces = [(ci, hd) for ci in range(n // HG_CHUNK) for hd in range(HG_HEADS)]
    gens = {}
    stage_cost = (260, 40, 40, 40)

    def advance(i):
        if i not in gens:
            gens[i] = hg_piece(*pieces[i])
        next(gens[i], None)

    for r in range(len(pieces) + len(stage_cost) - 1):
        for stage in (1, 2, 3, 0):
            i = r - stage
            if 0 <= i < len(pieces):
                vpu.append((stage_cost[stage], lambda i=i: advance(i)))

    for thunk in _interleave([mxu, vpu]):
        thunk()


def _layer_kernel(blocks_per_seq,
                  xa_ref, xc_ref, mixg_ref, win_ref, h0_ref, tail0_ref, st0_ref,
                  cw_ref, cb_ref, wg_ref, br_ref, bi_ref, lam_ref, rgn_ref, lb_ref, hgn_ref,
                  wo_ref, fg_ref, wgu_ref, wd_ref, ng_ref,
                  o_ref, p0, p1, y0, y1, act, h1_s, gates_s, xconv_s, xbuf, h_s, st_s):
    j = pl.program_id(0)
    n = BLOCK_ROWS

    @pl.when(j == 0)
    def _():
        p1[...] = jnp.zeros_like(p1)
        y0[...] = jnp.zeros_like(y0)
        y1[...] = jnp.zeros_like(y1)
        xbuf[0:SUBLANES, :] = jnp.zeros((SUBLANES, D_RG), _F32)
        h_s[...] = jnp.zeros_like(h_s)
        st_s[...] = jnp.zeros_like(st_s)

    w = dict(prm=(cw_ref[...], cb_ref[...], wg_ref[...], br_ref[...], bi_ref[...], lam_ref[...], rgn_ref[...]),
             lb=_hg_lower_bound(lb_ref[...]), hgn=hgn_ref[...], mixg=mixg_ref[...], w_in=win_ref,
             w_out=wo_ref, ffn_g=fg_ref[...], w_gu=wgu_ref, w_down=wd_ref, final_g=ng_ref[...])
    s = dict(h1=h1_s, act=act, gates=gates_s, xconv=xconv_s, xbuf=xbuf, h=h_s, st=st_s)
    lo, hi = pl.ds(0, n), pl.ds(n, n)

    _tick(xa_ref.at[lo], xc_ref.at[lo], o_ref.at[lo], p0, p1, y1, y0, w, s)

    @pl.when(lax.rem(2 * j, blocks_per_seq) == 0)
    def _():
        h_s[...] = h0_ref[...]
        xbuf[0:SUBLANES, :] = tail0_ref[...]
        st_s[...] = st0_ref[...]

    _tick(xa_ref.at[hi], xc_ref.at[hi], o_ref.at[hi], p1, p0, y0, y1, w, s)


def _layer(x2d, seq, mixg, w_in, h0, tail0, st0, mix_prm, hgn, wo, fg, wgu, wd, ng):
    t, d = x2d.shape
    n = BLOCK_ROWS
    assert t % (2 * n) == 0 and seq % (2 * n) == 0
    steps = t // (2 * n)
    const = lambda a: pl.BlockSpec(a.shape, lambda j, nd=a.ndim: (0,) * nd, pipeline_mode=pl.Buffered(1))
    small = (h0, tail0, st0) + tuple(mix_prm) + (hgn,)
    in_specs = ([pl.BlockSpec((2 * n, d), lambda j: (jnp.minimum(j, steps - 1), 0)),
                 pl.BlockSpec((2 * n, d), lambda j: (jnp.maximum(j - 1, 0), 0)),
                 const(mixg), const(w_in)] + [const(a) for a in small]
                + [const(wo), const(fg), const(wgu), const(wd), const(ng)])
    return pl.pallas_call(
        functools.partial(_layer_kernel, seq // n),
        grid=(steps + 1,),
        in_specs=in_specs,
        out_specs=pl.BlockSpec((2 * n, d), lambda j: (jnp.maximum(j - 1, 0), 0)),
        out_shape=jax.ShapeDtypeStruct((t, d), _F32),
        scratch_shapes=[
            pltpu.VMEM((n, D_IN), _F32), pltpu.VMEM((n, D_IN), _F32),
            pltpu.VMEM((n, D_MODEL), _BF16), pltpu.VMEM((n, D_MODEL), _BF16),
            pltpu.VMEM((n, D_FF), _BF16),
            pltpu.VMEM((n, D_MODEL), _F32),
            pltpu.VMEM((n, 2 * D_RG), _F32),
            pltpu.VMEM((n, D_RG), _F32),
            pltpu.VMEM((n + SUBLANES, D_RG), _F32),
            pltpu.VMEM((SUBLANES, D_RG), _F32),
            pltpu.VMEM((HG_HEADS, HG_HEAD_DIM, HG_HEAD_DIM), _F32),
        ],
        compiler_params=pltpu.CompilerParams(
            dimension_semantics=("arbitrary",), vmem_limit_bytes=VMEM_LIMIT),
        name="layer",
    )(x2d, x2d, mixg, w_in, *small, wo, fg, wgu, wd, ng)


def _pair_gate_weights(w_r, w_i):
    def pairs(w):
        h, d, _ = w.shape
        w = w.reshape(h // 2, 2, d, d)
        eye = jnp.eye(2, dtype=w.dtype)
        return (eye[None, :, None, :, None] * w[:, :, :, None, :]).reshape(h // 2, 2 * d, 2 * d)
    return jnp.concatenate([pairs(w_r), pairs(w_i)], axis=2)


def kernel(x, meta_tokens, mix_norm_g, w_in, conv_w, conv_b, w_rgate, b_rgate, w_igate, b_igate,
           lru_lambda, rg_norm_g, hg_lower_bound, hg_norm_g, w_out, ffn_norm_g, w_gate_up, w_down,
           final_norm_g):
    b, s, d = x.shape
    row = lambda a: a.reshape(1, -1).astype(_F32)
    w_in_b = w_in[0].astype(_BF16)
    wg = _pair_gate_weights(w_rgate[0], w_igate[0]).astype(_BF16)
    mixg = row(mix_norm_g[0])
    mix_prm = (conv_w[0].astype(_F32), row(conv_b[0]), wg, row(b_rgate[0]), row(b_igate[0]),
               row(lru_lambda[0]), row(rg_norm_g[0]), hg_lower_bound.astype(_F32))

    p_meta = _norm_proj(meta_tokens.astype(x.dtype), mixg, w_in_b)
    h0, tail0, st0 = _meta_state(p_meta, *mix_prm)
    out = _layer(x.reshape(b * s, d), s, mixg, w_in_b, h0, tail0, st0, mix_prm, row(hg_norm_g[0]),
                 w_out[0].astype(_BF16), row(ffn_norm_g[0]), w_gate_up[0].astype(_BF16),
                 w_down[0].astype(_BF16), row(final_norm_g))
    return out.reshape(b, s, d)
```

```python
import functools

import jax
import jax.numpy as jnp
from jax import lax
from jax.experimental import pallas as pl
from jax.experimental.pallas import tpu as pltpu

D_MODEL = 1024
N_META = 16
D_RG = 512
CONV_W = 4
LRU_C = 8.0
D_HG = 512
HG_HEAD_DIM = 128
HG_HEADS = 4
D_IN = 2 * D_RG + 4 * D_HG
D_FF = 2816
EPS = 1e-6

GATE_PAIR = 128
HG_CHUNK = 64
SUBLANES = 8
LANES = 128
BLOCK_ROWS = 256
IN_TILE = 512
FF_TILE = 256
VMEM_LIMIT = 60 * 1024 * 1024

_F32 = jnp.float32
_BF16 = jnp.bfloat16
_NT = (((1,), (1,)), ((), ()))
_TN = (((0,), (0,)), ((), ()))


def _rms(x, g):
    return x * lax.rsqrt(jnp.mean(x * x, axis=-1, keepdims=True) + EPS) * g


def _silu(x):
    hx = 0.5 * x
    return hx + hx * jnp.tanh(hx)


def _scaled_sigmoid(x, scale):
    hs = 0.5 * scale
    return hs + hs * jnp.tanh(0.5 * x)


def _gelu_tanh(x):
    c = 0.7978845608028654
    t = jnp.tanh(x * (c + (c * 0.044715) * (x * x)))
    return (0.5 * x) * (1.0 + t)


def _dot(a, b):
    return lax.dot_general(a, b, (((1,), (0,)), ((), ())), preferred_element_type=_F32)


def _slab_roll(x, shift):
    groups = [pltpu.roll(x[i:i + SUBLANES], shift, 0) for i in range(0, x.shape[0], SUBLANES)]
    return jnp.concatenate(groups, axis=0)


def _rg_gates(xr, xbuf, prm):
    conv_w, conv_b, w_gate = prm[0], prm[1], prm[2]
    n = xr.shape[0]
    xbuf[SUBLANES:SUBLANES + n, :] = xr
    xc = conv_b + conv_w[3:4, :] * xr
    for j in range(1, CONV_W):
        xc = xc + conv_w[CONV_W - 1 - j:CONV_W - j, :] * xbuf[SUBLANES - j:SUBLANES - j + n, :]
    xbuf[0:SUBLANES, :] = xr[n - SUBLANES:n, :]
    xb = xc.astype(_BF16)
    parts = [_dot(xb[:, p * GATE_PAIR:(p + 1) * GATE_PAIR], w_gate[p]) for p in range(D_RG // GATE_PAIR)]
    gates = jnp.concatenate([part[:, :GATE_PAIR] for part in parts] + [part[:, GATE_PAIR:] for part in parts],
                            axis=1)
    return xc, gates


def _rg_rows(xc, gates, gr, h_ref, prm):
    b_r, b_i, lam, norm_g = prm[3], prm[4], prm[5], prm[6]
    n = xc.shape[0]
    z = -lam
    softplus = jnp.maximum(z, 0.0) + jnp.log1p(jnp.exp(-jnp.abs(z)))
    log_a = _scaled_sigmoid(gates[:, :D_RG] + b_r, -LRU_C * softplus)
    a = jnp.exp(log_a)
    ix = _scaled_sigmoid(gates[:, D_RG:] + b_i, xc)
    bx = jnp.sqrt(-jnp.tanh(log_a) * (a * a + 1.0)) * ix

    pos = lax.broadcasted_iota(jnp.int32, (n, 1), 0) & (SUBLANES - 1)
    for s in (1, 2, 4):
        keep = pos >= s
        a_prev = _slab_roll(a, s)
        b_prev = _slab_roll(bx, s)
        bx = jnp.where(keep, a * b_prev + bx, bx)
        a = jnp.where(keep, a * a_prev, a)
    h = h_ref[...]
    hs = []
    for g in range(n // SUBLANES):
        rows = slice(g * SUBLANES, (g + 1) * SUBLANES)
        hg = a[rows] * h + bx[rows]
        hs.append(hg)
        h = jnp.broadcast_to(hg[SUBLANES - 1:SUBLANES, :], (SUBLANES, D_RG))
    h_ref[...] = h
    y = _gelu_tanh(gr) * jnp.concatenate(hs, axis=0)
    return _rms(y, norm_g)


def _hg_stages(hq, hf, hi, lb, st_ref, head, finish, valid=None):
    c = HG_CHUNK
    q = _silu(hq)
    half = 0.5 * (1.0 - lb)
    f = (lb + half) + half * jnp.tanh(0.5 * hf)
    v = hi
    if valid is not None:
        q = jnp.where(valid, q, 0.0)
        f = jnp.where(valid, f, 1.0)
        v = jnp.where(valid, v, 0.0)
    k = 1.0 - f
    vb = v.astype(_BF16)

    row = lax.broadcasted_iota(jnp.int32, (c, 1), 0)
    pos = row & (SUBLANES - 1)

    kd = k
    vd = v
    o_band = jnp.sum(q * kd, axis=-1, keepdims=True) * vd
    for d in range(1, SUBLANES):
        kd = _slab_roll(kd, 1) * f
        vd = _slab_roll(vd, 1)
        a = jnp.sum(q * kd, axis=-1, keepdims=True)
        o_band = o_band + jnp.where(pos >= d, a, 0.0) * vd

    tot = f
    pre = f
    suf = jnp.ones_like(f)
    m = 1
    while m < SUBLANES:
        upper = (row & m) != 0
        sib = jnp.where(upper, _slab_roll(tot, m), _slab_roll(tot, SUBLANES - m))
        pre = jnp.where(upper, pre * sib, pre)
        suf = jnp.where(upper, suf, suf * sib)
        tot = tot * sib
        m *= 2

    ns = c // SUBLANES
    slabs = lambda a: [a[i * SUBLANES:(i + 1) * SUBLANES] for i in range(ns)]
    q_s, k_s, pre_s, suf_s, tot_s = slabs(q), slabs(k), slabs(pre), slabs(suf), slabs(tot)
    zero = jnp.zeros((SUBLANES, q.shape[1]), _F32)
    operands = []
    g = 1
    while g < ns:
        lhs = jnp.concatenate([q_s[i] * pre_s[i] if i & g else zero for i in range(ns)], axis=0)
        rhs = jnp.concatenate([zero if i & g else k_s[i] * suf_s[i] for i in range(ns)], axis=0)
        operands.append((g * SUBLANES, lhs.astype(_BF16), rhs.astype(_BF16)))
        sib = [tot_s[i ^ g] for i in range(ns)]
        pre_s = [pre_s[i] * sib[i] if i & g else pre_s[i] for i in range(ns)]
        suf_s = [suf_s[i] if i & g else suf_s[i] * sib[i] for i in range(ns)]
        tot_s = [tot_s[i] * sib[i] for i in range(ns)]
        g *= 2
    qp = (q * jnp.concatenate(pre_s, axis=0)).astype(_BF16)
    ks = (k * jnp.concatenate(suf_s, axis=0)).astype(_BF16)
    tot = tot_s[0]
    yield

    st = st_ref[head]
    prods = [(m, lax.dot_general(lhs, rhs, _NT, preferred_element_type=_F32)) for m, lhs, rhs in operands]
    inter = lax.dot_general(qp, st.astype(_BF16), _NT, preferred_element_type=_F32)
    upd = lax.dot_general(vb, ks, _TN, preferred_element_type=_F32)
    yield

    ti = lax.broadcasted_iota(jnp.int32, (c, c), 0)
    si = lax.broadcasted_iota(jnp.int32, (c, c), 1)
    att = None
    for m, prod in prods:
        if 2 * m < c:
            shift = (2 * m).bit_length() - 1
            prod = jnp.where((ti >> shift) == (si >> shift), prod, 0.0)
        att = prod if att is None else att + prod
    intra = _dot(att.astype(_BF16), vb)
    st_ref[head] = st * tot[0:1, :] + upd
    yield

    finish(inter + intra + o_band)


def _hg_chunk(hq, hf, hi, lb, st_ref, head, valid=None):
    out = []
    for _ in _hg_stages(hq, hf, hi, lb, st_ref, head, out.append, valid=valid):
        pass
    return out[0]


def _hg_lower_bound(lb2):
    m = jnp.maximum(lb2[0:1, :], lb2[1:2, :])
    e0 = jnp.exp(lb2[0:1, :] - m)
    e1 = jnp.exp(lb2[1:2, :] - m)
    return e0 / (e0 + e1)


def _norm_proj_kernel(x_ref, g_ref, w_ref, o_ref):
    u = _rms(x_ref[...], g_ref[...]).astype(_BF16)
    o_ref[...] = _dot(u, w_ref[...])


def _norm_proj(x2d, g, w_bf16):
    t, d = x2d.shape
    n = w_bf16.shape[1]
    full = lambda a: pl.BlockSpec(a.shape, lambda: (0,) * a.ndim)
    return pl.pallas_call(
        _norm_proj_kernel,
        in_specs=[full(x2d), full(g), full(w_bf16)],
        out_specs=pl.BlockSpec((t, n), lambda: (0, 0)),
        out_shape=jax.ShapeDtypeStruct((t, n), _F32),
        compiler_params=pltpu.CompilerParams(vmem_limit_bytes=VMEM_LIMIT),
        name="meta_proj",
    )(x2d, g, w_bf16)


def _meta_state_kernel(p_ref, cw_ref, cb_ref, wg_ref, br_ref, bi_ref, lam_ref, rgn_ref, lb_ref,
                       h_out, tail_out, st_out, xbuf):
    n = N_META
    xbuf[...] = jnp.zeros_like(xbuf)
    h_out[...] = jnp.zeros_like(h_out)
    prm = (cw_ref[...], cb_ref[...], wg_ref[...], br_ref[...], bi_ref[...], lam_ref[...], rgn_ref[...])
    xc, gates = _rg_gates(p_ref[:, 0:D_RG], xbuf, prm)
    _rg_rows(xc, gates, p_ref[:, D_RG:2 * D_RG], h_out, prm)
    tail_out[...] = xbuf[0:SUBLANES, :]

    st_out[...] = jnp.zeros_like(st_out)
    lb = _hg_lower_bound(lb_ref[...])
    pad = HG_CHUNK - n
    valid = lax.broadcasted_iota(jnp.int32, (HG_CHUNK, 1), 0) >= pad
    zeros = jnp.zeros((pad, HG_HEAD_DIM), _F32)
    base = 2 * D_RG
    for hd in range(HG_HEADS):
        lo = hd * HG_HEAD_DIM
        cols = lambda part: slice(base + part * D_HG + lo, base + part * D_HG + lo + HG_HEAD_DIM)
        hq = jnp.concatenate([zeros, p_ref[:, cols(0)]], axis=0)
        hf = jnp.concatenate([zeros, p_ref[:, cols(1)]], axis=0)
        hi = jnp.concatenate([zeros, p_ref[:, cols(2)]], axis=0)
        _hg_chunk(hq, hf, hi, lb[:, lo:lo + HG_HEAD_DIM], st_out, hd, valid=valid)


def _meta_state(p_meta, cw, cb, wg, br, bi, lam, rgn, lb2):
    full = lambda a: pl.BlockSpec(a.shape, lambda: (0,) * a.ndim)
    args = (p_meta, cw, cb, wg, br, bi, lam, rgn, lb2)
    out_shapes = (
        jax.ShapeDtypeStruct((SUBLANES, D_RG), _F32),
        jax.ShapeDtypeStruct((SUBLANES, D_RG), _F32),
        jax.ShapeDtypeStruct((HG_HEADS, HG_HEAD_DIM, HG_HEAD_DIM), _F32),
    )
    return pl.pallas_call(
        _meta_state_kernel,
        in_specs=[full(a) for a in args],
        out_specs=tuple(pl.BlockSpec(s.shape, lambda n=len(s.shape): (0,) * n) for s in out_shapes),
        out_shape=out_shapes,
        scratch_shapes=[pltpu.VMEM((N_META + SUBLANES, D_RG), _F32)],
        name="meta_state",
    )(*args)


def _interleave(streams):
    totals = [sum(c for c, _ in st) for st in streams]
    pos = [0] * len(streams)
    done = [0.0] * len(streams)
    order = []
    while any(p < len(st) for p, st in zip(pos, streams)):
        k = min((i for i in range(len(streams)) if pos[i] < len(streams[i])),
                key=lambda i: done[i] / totals[i])
        cost, thunk = streams[k][pos[k]]
        order.append(thunk)
        pos[k] += 1
        done[k] += cost
    return order


def _tick(xa, xc, o_w, p_w, p_r, y_w, y_r, w, s):
    n = BLOCK_ROWS
    prm, lb, hgn = w["prm"], w["lb"], w["hgn"]
    mxu, vpu = [], []

    val = {}

    def a_norm():
        val["u"] = _rms(xa[...], w["mixg"]).astype(_BF16)
    mxu.append((400, a_norm))
    for t in range(D_IN // IN_TILE):
        def a_tile(t=t):
            cols = slice(t * IN_TILE, (t + 1) * IN_TILE)
            p_w[:, cols] = _dot(val["u"], w["w_in"][:, cols])
        mxu.append((512, a_tile))

    for t in range(D_MODEL // FF_TILE):
        def c_proj(t=t):
            if "y" not in val:
                val["y"] = y_r[...]
            cols = slice(t * FF_TILE, (t + 1) * FF_TILE)
            s["h1"][:, cols] = xc[:, cols] + _dot(val["y"], w["w_out"][:, cols])
        mxu.append((300, c_proj))

    def c_rms():
        val["v"] = _rms(s["h1"][...], w["ffn_g"]).astype(_BF16)
    mxu.append((250, c_rms))
    pending = {}
    n_ff = D_FF // FF_TILE
    for j in range(n_ff + 1):
        def c_gate_up(j=j):
            lo = j * FF_TILE
            pending[j] = (_dot(val["v"], w["w_gu"][:, lo:lo + FF_TILE]),
                          _dot(val["v"], w["w_gu"][:, D_FF + lo:D_FF + lo + FF_TILE]))

        def c_act(j=j - 1):
            gate, up = pending.pop(j)
            s["act"][:, j * FF_TILE:(j + 1) * FF_TILE] = (_silu(gate) * up).astype(_BF16)
        if j < n_ff:
            mxu.append((512, c_gate_up))
        if j > 0:
            mxu.append((100, c_act))
    for t in range(D_MODEL // FF_TILE):
        def c_down(t=t):
            if "act" not in val:
                val["act"] = s["act"][...]
            cols = slice(t * FF_TILE, (t + 1) * FF_TILE)
            o_w[:, cols] = s["h1"][:, cols] + _dot(val["act"], w["w_down"][:, cols])
        mxu.append((750, c_down))

    def c_norm():
        o_w[...] = _rms(o_w[...], w["final_g"])
    mxu.append((400, c_norm))

    def b_gates():
        xconv, gates = _rg_gates(p_r[:, 0:D_RG], s["xbuf"], prm)
        s["xconv"][...] = xconv
        s["gates"][...] = gates
    vpu.append((900, b_gates))
    for g in range(n // HG_CHUNK):
        def b_rg(g=g):
            rows = slice(g * HG_CHUNK, (g + 1) * HG_CHUNK)
            y = _rg_rows(s["xconv"][rows, :], s["gates"][rows, :], p_r[rows, D_RG:2 * D_RG], s["h"], prm)
            y_w[rows, 0:D_RG] = y.astype(_BF16)
        vpu.append((750, b_rg))
    base = 2 * D_RG

    def hg_piece(ci, hd):
        rows = slice(ci * HG_CHUNK, (ci + 1) * HG_CHUNK)
        lo = hd * HG_HEAD_DIM
        col = lambda part: slice(base + part * D_HG + lo, base + part * D_HG + lo + HG_HEAD_DIM)

        def finish(o):
            o = _rms(o, hgn) * _silu(p_r[rows, col(3)])
            y_w[rows, D_RG + lo:D_RG + lo + HG_HEAD_DIM] = o.astype(_BF16)
        return _hg_stages(p_r[rows, col(0)], p_r[rows, col(1)], p_r[rows, col(2)],
                          lb[:, lo:lo + HG_HEAD_DIM], s["st"], hd, finish)

    pieces = [(ci, hd) for ci in range(n // HG_CHUNK) for hd in range(HG_HEADS)]
    gens = {}
    stage_cost = (260, 40, 40, 40)

    def advance(i):
        if i not in gens:
            gens[i] = hg_piece(*pieces[i])
        next(gens[i], None)

    for r in range(len(pieces) + len(stage_cost) - 1):
        for stage in (1, 2, 3, 0):
            i = r - stage
            if 0 <= i < len(pieces):
                vpu.append((stage_cost[stage], lambda i=i: advance(i)))

    for thunk in _interleave([mxu, vpu]):
        thunk()


def _layer_kernel(blocks_per_seq,
                  xa_ref, xc_ref, mixg_ref, win_ref, h0_ref, tail0_ref, st0_ref,
                  cw_ref, cb_ref, wg_ref, br_ref, bi_ref, lam_ref, rgn_ref, lb_ref, hgn_ref,
                  wo_ref, fg_ref, wgu_ref, wd_ref, ng_ref,
                  o_ref, p0, p1, y0, y1, act, h1_s, gates_s, xconv_s, xbuf, h_s, st_s):
    j = pl.program_id(0)
    n = BLOCK_ROWS

    @pl.when(j == 0)
    def _():
        p1[...] = jnp.zeros_like(p1)
        y0[...] = jnp.zeros_like(y0)
        y1[...] = jnp.zeros_like(y1)
        xbuf[0:SUBLANES, :] = jnp.zeros((SUBLANES, D_RG), _F32)
        h_s[...] = jnp.zeros_like(h_s)
        st_s[...] = jnp.zeros_like(st_s)

    w = dict(prm=(cw_ref[...], cb_ref[...], wg_ref[...], br_ref[...], bi_ref[...], lam_ref[...], rgn_ref[...]),
             lb=_hg_lower_bound(lb_ref[...]), hgn=hgn_ref[...], mixg=mixg_ref[...], w_in=win_ref,
             w_out=wo_ref, ffn_g=fg_ref[...], w_gu=wgu_ref, w_down=wd_ref, final_g=ng_ref[...])
    s = dict(h1=h1_s, act=act, gates=gates_s, xconv=xconv_s, xbuf=xbuf, h=h_s, st=st_s)
    lo, hi = pl.ds(0, n), pl.ds(n, n)

    _tick(xa_ref.at[lo], xc_ref.at[lo], o_ref.at[lo], p0, p1, y1, y0, w, s)

    @pl.when(lax.rem(2 * j, blocks_per_seq) == 0)
    def _():
        h_s[...] = h0_ref[...]
        xbuf[0:SUBLANES, :] = tail0_ref[...]
        st_s[...] = st0_ref[...]

    _tick(xa_ref.at[hi], xc_ref.at[hi], o_ref.at[hi], p1, p0, y0, y1, w, s)


def _layer(x2d, seq, mixg, w_in, h0, tail0, st0, mix_prm, hgn, wo, fg, wgu, wd, ng):
    t, d = x2d.shape
    n = BLOCK_ROWS
    assert t % (2 * n) == 0 and seq % (2 * n) == 0
    steps = t // (2 * n)
    const = lambda a: pl.BlockSpec(a.shape, lambda j, nd=a.ndim: (0,) * nd, pipeline_mode=pl.Buffered(1))
    small = (h0, tail0, st0) + tuple(mix_prm) + (hgn,)
    in_specs = ([pl.BlockSpec((2 * n, d), lambda j: (jnp.minimum(j, steps - 1), 0)),
                 pl.BlockSpec((2 * n, d), lambda j: (jnp.maximum(j - 1, 0), 0)),
                 const(mixg), const(w_in)] + [const(a) for a in small]
                + [const(wo), const(fg), const(wgu), const(wd), const(ng)])
    return pl.pallas_call(
        functools.partial(_layer_kernel, seq // n),
        grid=(steps + 1,),
        in_specs=in_specs,
        out_specs=pl.BlockSpec((2 * n, d), lambda j: (jnp.maximum(j - 1, 0), 0)),
        out_shape=jax.ShapeDtypeStruct((t, d), _F32),
        scratch_shapes=[
            pltpu.VMEM((n, D_IN), _F32), pltpu.VMEM((n, D_IN), _F32),
            pltpu.VMEM((n, D_MODEL), _BF16), pltpu.VMEM((n, D_MODEL), _BF16),
            pltpu.VMEM((n, D_FF), _BF16),
            pltpu.VMEM((n, D_MODEL), _F32),
            pltpu.VMEM((n, 2 * D_RG), _F32),
            pltpu.VMEM((n, D_RG), _F32),
            pltpu.VMEM((n + SUBLANES, D_RG), _F32),
            pltpu.VMEM((SUBLANES, D_RG), _F32),
            pltpu.VMEM((HG_HEADS, HG_HEAD_DIM, HG_HEAD_DIM), _F32),
        ],
        compiler_params=pltpu.CompilerParams(
            dimension_semantics=("arbitrary",), vmem_limit_bytes=VMEM_LIMIT),
        name="layer",
    )(x2d, x2d, mixg, w_in, *small, wo, fg, wgu, wd, ng)


def _pair_gate_weights(w_r, w_i):
    def pairs(w):
        h, d, _ = w.shape
        w = w.reshape(h // 2, 2, d, d)
        eye = jnp.eye(2, dtype=w.dtype)
        return (eye[None, :, None, :, None] * w[:, :, :, None, :]).reshape(h // 2, 2 * d, 2 * d)
    return jnp.concatenate([pairs(w_r), pairs(w_i)], axis=2)


def kernel(x, meta_tokens, mix_norm_g, w_in, conv_w, conv_b, w_rgate, b_rgate, w_igate, b_igate,
           lru_lambda, rg_norm_g, hg_lower_bound, hg_norm_g, w_out, ffn_norm_g, w_gate_up, w_down,
           final_norm_g):
    b, s, d = x.shape
    row = lambda a: a.reshape(1, -1).astype(_F32)
    skew = lambda wt: jnp.pad(wt.astype(_BF16), ((0, 0), (0, LANES)))
    w_in_b = skew(w_in[0])
    wg = _pair_gate_weights(w_rgate[0], w_igate[0]).astype(_BF16)
    mixg = row(mix_norm_g[0])
    mix_prm = (conv_w[0].astype(_F32), row(conv_b[0]), wg, row(b_rgate[0]), row(b_igate[0]),
               row(lru_lambda[0]), row(rg_norm_g[0]), hg_lower_bound.astype(_F32))

    p_meta = _norm_proj(meta_tokens.astype(x.dtype), mixg, w_in_b)
    h0, tail0, st0 = _meta_state(p_meta, *mix_prm)
    out = _layer(x.reshape(b * s, d), s, mixg, w_in_b, h0, tail0, st0, mix_prm, row(hg_norm_g[0]),
                 skew(w_out[0]), row(ffn_norm_g[0]), w_gate_up[0].astype(_BF16),
                 skew(w_down[0]), row(final_norm_g))
    return out.reshape(b, s, d)
```

```python
import functools

import jax
import jax.numpy as jnp
from jax import lax
from jax.experimental import pallas as pl
from jax.experimental.pallas import tpu as pltpu

D_MODEL = 1024
N_META = 16
D_RG = 512
CONV_W = 4
LRU_C = 8.0
D_HG = 512
HG_HEAD_DIM = 128
HG_HEADS = 4
D_IN = 2 * D_RG + 4 * D_HG
D_FF = 2816
EPS = 1e-6

GATE_PAIR = 128
HG_CHUNK = 64
SUBLANES = 8
LANES = 128
BLOCK_ROWS = 256
IN_TILE = 512
EARLY_A_TILES = (2 * D_RG) // IN_TILE
FF_TILE = 256
VMEM_LIMIT = 60 * 1024 * 1024

_F32 = jnp.float32
_BF16 = jnp.bfloat16
_NT = (((1,), (1,)), ((), ()))
_TN = (((0,), (0,)), ((), ()))


def _rms(x, g):
    return x * lax.rsqrt(jnp.mean(x * x, axis=-1, keepdims=True) + EPS) * g


def _silu(x):
    hx = 0.5 * x
    return hx + hx * jnp.tanh(hx)


def _scaled_sigmoid(x, scale):
    hs = 0.5 * scale
    return hs + hs * jnp.tanh(0.5 * x)


def _gelu_tanh(x):
    c = 0.7978845608028654
    t = jnp.tanh(x * (c + (c * 0.044715) * (x * x)))
    return (0.5 * x) * (1.0 + t)


def _dot(a, b):
    return lax.dot_general(a, b, (((1,), (0,)), ((), ())), preferred_element_type=_F32)


def _slab_roll(x, shift):
    groups = [pltpu.roll(x[i:i + SUBLANES], shift, 0) for i in range(0, x.shape[0], SUBLANES)]
    return jnp.concatenate(groups, axis=0)


def _rg_gates(xr, xbuf, prm):
    conv_w, conv_b, w_gate = prm[0], prm[1], prm[2]
    n = xr.shape[0]
    xbuf[SUBLANES:SUBLANES + n, :] = xr
    xc = conv_b + conv_w[3:4, :] * xr
    for j in range(1, CONV_W):
        xc = xc + conv_w[CONV_W - 1 - j:CONV_W - j, :] * xbuf[SUBLANES - j:SUBLANES - j + n, :]
    xbuf[0:SUBLANES, :] = xr[n - SUBLANES:n, :]
    xb = xc.astype(_BF16)
    parts = [_dot(xb[:, p * GATE_PAIR:(p + 1) * GATE_PAIR], w_gate[p]) for p in range(D_RG // GATE_PAIR)]
    gates = jnp.concatenate([part[:, :GATE_PAIR] for part in parts] + [part[:, GATE_PAIR:] for part in parts],
                            axis=1)
    return xc, gates


def _rg_rows(xc, gates, gr, h_ref, prm):
    b_r, b_i, lam, norm_g = prm[3], prm[4], prm[5], prm[6]
    n = xc.shape[0]
    z = -lam
    softplus = jnp.maximum(z, 0.0) + jnp.log1p(jnp.exp(-jnp.abs(z)))
    log_a = _scaled_sigmoid(gates[:, :D_RG] + b_r, -LRU_C * softplus)
    a = jnp.exp(log_a)
    ix = _scaled_sigmoid(gates[:, D_RG:] + b_i, xc)
    bx = jnp.sqrt(-jnp.tanh(log_a) * (a * a + 1.0)) * ix

    pos = lax.broadcasted_iota(jnp.int32, (n, 1), 0) & (SUBLANES - 1)
    for s in (1, 2, 4):
        keep = pos >= s
        a_prev = _slab_roll(a, s)
        b_prev = _slab_roll(bx, s)
        bx = jnp.where(keep, a * b_prev + bx, bx)
        a = jnp.where(keep, a * a_prev, a)
    h = h_ref[...]
    hs = []
    for g in range(n // SUBLANES):
        rows = slice(g * SUBLANES, (g + 1) * SUBLANES)
        hg = a[rows] * h + bx[rows]
        hs.append(hg)
        h = jnp.broadcast_to(hg[SUBLANES - 1:SUBLANES, :], (SUBLANES, D_RG))
    h_ref[...] = h
    y = _gelu_tanh(gr) * jnp.concatenate(hs, axis=0)
    return _rms(y, norm_g)


def _hg_stages(hq, hf, hi, lb, st_ref, head, finish, valid=None):
    c = HG_CHUNK
    q = _silu(hq)
    half = 0.5 * (1.0 - lb)
    f = (lb + half) + half * jnp.tanh(0.5 * hf)
    v = hi
    if valid is not None:
        q = jnp.where(valid, q, 0.0)
        f = jnp.where(valid, f, 1.0)
        v = jnp.where(valid, v, 0.0)
    k = 1.0 - f
    vb = v.astype(_BF16)

    row = lax.broadcasted_iota(jnp.int32, (c, 1), 0)
    pos = row & (SUBLANES - 1)

    kd = k
    vd = v
    o_band = jnp.sum(q * kd, axis=-1, keepdims=True) * vd
    for d in range(1, SUBLANES):
        kd = _slab_roll(kd, 1) * f
        vd = _slab_roll(vd, 1)
        a = jnp.sum(q * kd, axis=-1, keepdims=True)
        o_band = o_band + jnp.where(pos >= d, a, 0.0) * vd

    tot = f
    pre = f
    suf = jnp.ones_like(f)
    m = 1
    while m < SUBLANES:
        upper = (row & m) != 0
        sib = jnp.where(upper, _slab_roll(tot, m), _slab_roll(tot, SUBLANES - m))
        pre = jnp.where(upper, pre * sib, pre)
        suf = jnp.where(upper, suf, suf * sib)
        tot = tot * sib
        m *= 2

    ns = c // SUBLANES
    slabs = lambda a: [a[i * SUBLANES:(i + 1) * SUBLANES] for i in range(ns)]
    q_s, k_s, pre_s, suf_s, tot_s = slabs(q), slabs(k), slabs(pre), slabs(suf), slabs(tot)
    zero = jnp.zeros((SUBLANES, q.shape[1]), _F32)
    operands = []
    g = 1
    while g < ns:
        lhs = jnp.concatenate([q_s[i] * pre_s[i] if i & g else zero for i in range(ns)], axis=0)
        rhs = jnp.concatenate([zero if i & g else k_s[i] * suf_s[i] for i in range(ns)], axis=0)
        operands.append((g * SUBLANES, lhs.astype(_BF16), rhs.astype(_BF16)))
        sib = [tot_s[i ^ g] for i in range(ns)]
        pre_s = [pre_s[i] * sib[i] if i & g else pre_s[i] for i in range(ns)]
        suf_s = [suf_s[i] if i & g else suf_s[i] * sib[i] for i in range(ns)]
        tot_s = [tot_s[i] * sib[i] for i in range(ns)]
        g *= 2
    qp = (q * jnp.concatenate(pre_s, axis=0)).astype(_BF16)
    ks = (k * jnp.concatenate(suf_s, axis=0)).astype(_BF16)
    tot = tot_s[0]
    yield

    st = st_ref[head]
    prods = [(m, lax.dot_general(lhs, rhs, _NT, preferred_element_type=_F32)) for m, lhs, rhs in operands]
    inter = lax.dot_general(qp, st.astype(_BF16), _NT, preferred_element_type=_F32)
    upd = lax.dot_general(vb, ks, _TN, preferred_element_type=_F32)
    yield

    ti = lax.broadcasted_iota(jnp.int32, (c, c), 0)
    si = lax.broadcasted_iota(jnp.int32, (c, c), 1)
    att = None
    for m, prod in prods:
        if 2 * m < c:
            shift = (2 * m).bit_length() - 1
            prod = jnp.where((ti >> shift) == (si >> shift), prod, 0.0)
        att = prod if att is None else att + prod
    intra = _dot(att.astype(_BF16), vb)
    st_ref[head] = st * tot[0:1, :] + upd
    yield

    finish(inter + intra + o_band)


def _hg_chunk(hq, hf, hi, lb, st_ref, head, valid=None):
    out = []
    for _ in _hg_stages(hq, hf, hi, lb, st_ref, head, out.append, valid=valid):
        pass
    return out[0]


def _hg_lower_bound(lb2):
    m = jnp.maximum(lb2[0:1, :], lb2[1:2, :])
    e0 = jnp.exp(lb2[0:1, :] - m)
    e1 = jnp.exp(lb2[1:2, :] - m)
    return e0 / (e0 + e1)


def _norm_proj_kernel(x_ref, g_ref, w_ref, o_ref):
    u = _rms(x_ref[...], g_ref[...]).astype(_BF16)
    o_ref[...] = _dot(u, w_ref[...])


def _norm_proj(x2d, g, w_bf16):
    t, d = x2d.shape
    n = w_bf16.shape[1]
    full = lambda a: pl.BlockSpec(a.shape, lambda: (0,) * a.ndim)
    return pl.pallas_call(
        _norm_proj_kernel,
        in_specs=[full(x2d), full(g), full(w_bf16)],
        out_specs=pl.BlockSpec((t, n), lambda: (0, 0)),
        out_shape=jax.ShapeDtypeStruct((t, n), _F32),
        compiler_params=pltpu.CompilerParams(vmem_limit_bytes=VMEM_LIMIT),
        name="meta_proj",
    )(x2d, g, w_bf16)


def _meta_state_kernel(p_ref, cw_ref, cb_ref, wg_ref, br_ref, bi_ref, lam_ref, rgn_ref, lb_ref,
                       h_out, tail_out, st_out, xbuf):
    n = N_META
    xbuf[...] = jnp.zeros_like(xbuf)
    h_out[...] = jnp.zeros_like(h_out)
    prm = (cw_ref[...], cb_ref[...], wg_ref[...], br_ref[...], bi_ref[...], lam_ref[...], rgn_ref[...])
    xc, gates = _rg_gates(p_ref[:, 0:D_RG], xbuf, prm)
    _rg_rows(xc, gates, p_ref[:, D_RG:2 * D_RG], h_out, prm)
    tail_out[...] = xbuf[0:SUBLANES, :]

    st_out[...] = jnp.zeros_like(st_out)
    lb = _hg_lower_bound(lb_ref[...])
    pad = HG_CHUNK - n
    valid = lax.broadcasted_iota(jnp.int32, (HG_CHUNK, 1), 0) >= pad
    zeros = jnp.zeros((pad, HG_HEAD_DIM), _F32)
    base = 2 * D_RG
    for hd in range(HG_HEADS):
        lo = hd * HG_HEAD_DIM
        cols = lambda part: slice(base + part * D_HG + lo, base + part * D_HG + lo + HG_HEAD_DIM)
        hq = jnp.concatenate([zeros, p_ref[:, cols(0)]], axis=0)
        hf = jnp.concatenate([zeros, p_ref[:, cols(1)]], axis=0)
        hi = jnp.concatenate([zeros, p_ref[:, cols(2)]], axis=0)
        _hg_chunk(hq, hf, hi, lb[:, lo:lo + HG_HEAD_DIM], st_out, hd, valid=valid)


def _meta_state(p_meta, cw, cb, wg, br, bi, lam, rgn, lb2):
    full = lambda a: pl.BlockSpec(a.shape, lambda: (0,) * a.ndim)
    args = (p_meta, cw, cb, wg, br, bi, lam, rgn, lb2)
    out_shapes = (
        jax.ShapeDtypeStruct((SUBLANES, D_RG), _F32),
        jax.ShapeDtypeStruct((SUBLANES, D_RG), _F32),
        jax.ShapeDtypeStruct((HG_HEADS, HG_HEAD_DIM, HG_HEAD_DIM), _F32),
    )
    return pl.pallas_call(
        _meta_state_kernel,
        in_specs=[full(a) for a in args],
        out_specs=tuple(pl.BlockSpec(s.shape, lambda n=len(s.shape): (0,) * n) for s in out_shapes),
        out_shape=out_shapes,
        scratch_shapes=[pltpu.VMEM((N_META + SUBLANES, D_RG), _F32)],
        name="meta_state",
    )(*args)


def _interleave(streams):
    totals = [sum(c for c, _ in st) for st in streams]
    pos = [0] * len(streams)
    done = [0.0] * len(streams)
    order = []
    while any(p < len(st) for p, st in zip(pos, streams)):
        k = min((i for i in range(len(streams)) if pos[i] < len(streams[i])),
                key=lambda i: done[i] / totals[i])
        cost, thunk = streams[k][pos[k]]
        order.append(thunk)
        pos[k] += 1
        done[k] += cost
    return order


def _stage_a(xa, p_w, w):
    val = {}

    def a_norm():
        val["u"] = _rms(xa[...], w["mixg"]).astype(_BF16)
    pieces = [(400, a_norm)]
    for t in range(D_IN // IN_TILE):
        def a_tile(t=t):
            cols = slice(t * IN_TILE, (t + 1) * IN_TILE)
            p_w[:, cols] = _dot(val["u"], w["w_in"][:, cols])
        pieces.append((512, a_tile))
    return pieces


def _tick(a_head, a_tail, xc, o_w, p_r, y_w, y_r, w, s):
    n = BLOCK_ROWS
    prm, lb, hgn = w["prm"], w["lb"], w["hgn"]
    mxu, vpu = list(a_head), []
    val = {}

    for t in range(D_MODEL // FF_TILE):
        def c_proj(t=t):
            if "y" not in val:
                val["y"] = y_r[...]
            cols = slice(t * FF_TILE, (t + 1) * FF_TILE)
            s["h1"][:, cols] = xc[:, cols] + _dot(val["y"], w["w_out"][:, cols])
        mxu.append((300, c_proj))

    def c_rms():
        val["v"] = _rms(s["h1"][...], w["ffn_g"]).astype(_BF16)
    mxu.append((250, c_rms))
    pending = {}
    n_ff = D_FF // FF_TILE
    for j in range(n_ff + 1):
        def c_gate_up(j=j):
            lo = j * FF_TILE
            pending[j] = (_dot(val["v"], w["w_gu"][:, lo:lo + FF_TILE]),
                          _dot(val["v"], w["w_gu"][:, D_FF + lo:D_FF + lo + FF_TILE]))

        def c_act(j=j - 1):
            gate, up = pending.pop(j)
            s["act"][:, j * FF_TILE:(j + 1) * FF_TILE] = (_silu(gate) * up).astype(_BF16)
        if j < n_ff:
            mxu.append((512, c_gate_up))
        if j > 0:
            mxu.append((100, c_act))
    mxu.extend(a_tail[:1])
    for t in range(D_MODEL // FF_TILE):
        def c_down(t=t):
            if "act" not in val:
                val["act"] = s["act"][...]
            cols = slice(t * FF_TILE, (t + 1) * FF_TILE)
            o_w[:, cols] = s["h1"][:, cols] + _dot(val["act"], w["w_down"][:, cols])
        mxu.append((750, c_down))

    def c_norm():
        o_w[...] = _rms(o_w[...], w["final_g"])
    mxu.extend(a_tail[1:2])
    mxu.append((400, c_norm))
    mxu.extend(a_tail[2:])

    def b_gates():
        xconv, gates = _rg_gates(p_r[:, 0:D_RG], s["xbuf"], prm)
        s["xconv"][...] = xconv
        s["gates"][...] = gates
    vpu.append((900, b_gates))
    for g in range(n // HG_CHUNK):
        def b_rg(g=g):
            rows = slice(g * HG_CHUNK, (g + 1) * HG_CHUNK)
            y = _rg_rows(s["xconv"][rows, :], s["gates"][rows, :], p_r[rows, D_RG:2 * D_RG], s["h"], prm)
            y_w[rows, 0:D_RG] = y.astype(_BF16)
        vpu.append((750, b_rg))
    base = 2 * D_RG

    def hg_piece(ci, hd):
        rows = slice(ci * HG_CHUNK, (ci + 1) * HG_CHUNK)
        lo = hd * HG_HEAD_DIM
        col = lambda part: slice(base + part * D_HG + lo, base + part * D_HG + lo + HG_HEAD_DIM)

        def finish(o):
            o = _rms(o, hgn) * _silu(p_r[rows, col(3)])
            y_w[rows, D_RG + lo:D_RG + lo + HG_HEAD_DIM] = o.astype(_BF16)
        return _hg_stages(p_r[rows, col(0)], p_r[rows, col(1)], p_r[rows, col(2)],
                          lb[:, lo:lo + HG_HEAD_DIM], s["st"], hd, finish)

    pieces = [(ci, hd) for ci in range(n // HG_CHUNK) for hd in range(HG_HEADS)]
    gens = {}
    stage_cost = (260, 40, 40, 40)

    def advance(i):
        if i not in gens:
            gens[i] = hg_piece(*pieces[i])
        next(gens[i], None)

    for r in range(len(pieces) + len(stage_cost) - 1):
        for stage in (1, 2, 3, 0):
            i = r - stage
            if 0 <= i < len(pieces):
                vpu.append((stage_cost[stage], lambda i=i: advance(i)))

    for thunk in _interleave([mxu, vpu]):
        thunk()


def _layer_kernel(blocks_per_seq,
                  xa_ref, xc_ref, mixg_ref, win_ref, h0_ref, tail0_ref, st0_ref,
                  cw_ref, cb_ref, wg_ref, br_ref, bi_ref, lam_ref, rgn_ref, lb_ref, hgn_ref,
                  wo_ref, fg_ref, wgu_ref, wd_ref, ng_ref,
                  o_ref, p0, p1, y0, y1, act, h1_s, gates_s, xconv_s, xbuf, h_s, st_s):
    j = pl.program_id(0)
    n = BLOCK_ROWS

    @pl.when(j == 0)
    def _():
        p1[...] = jnp.zeros_like(p1)
        y0[...] = jnp.zeros_like(y0)
        y1[...] = jnp.zeros_like(y1)
        xbuf[0:SUBLANES, :] = jnp.zeros((SUBLANES, D_RG), _F32)
        h_s[...] = jnp.zeros_like(h_s)
        st_s[...] = jnp.zeros_like(st_s)

    w = dict(prm=(cw_ref[...], cb_ref[...], wg_ref[...], br_ref[...], bi_ref[...], lam_ref[...], rgn_ref[...]),
             lb=_hg_lower_bound(lb_ref[...]), hgn=hgn_ref[...], mixg=mixg_ref[...], w_in=win_ref,
             w_out=wo_ref, ffn_g=fg_ref[...], w_gu=wgu_ref, w_down=wd_ref, final_g=ng_ref[...])
    s = dict(h1=h1_s, act=act, gates=gates_s, xconv=xconv_s, xbuf=xbuf, h=h_s, st=st_s)
    lo, hi = pl.ds(0, n), pl.ds(n, n)

    a_even = _stage_a(xa_ref.at[lo], p0, w)
    a_odd = _stage_a(xa_ref.at[hi], p1, w)
    early = 1 + EARLY_A_TILES

    _tick(a_even, a_odd[:early], xc_ref.at[lo], o_ref.at[lo], p1, y1, y0, w, s)

    @pl.when(lax.rem(2 * j, blocks_per_seq) == 0)
    def _():
        h_s[...] = h0_ref[...]
        xbuf[0:SUBLANES, :] = tail0_ref[...]
        st_s[...] = st0_ref[...]

    _tick(a_odd[early:], [], xc_ref.at[hi], o_ref.at[hi], p0, y0, y1, w, s)


def _layer(x2d, seq, mixg, w_in, h0, tail0, st0, mix_prm, hgn, wo, fg, wgu, wd, ng):
    t, d = x2d.shape
    n = BLOCK_ROWS
    assert t % (2 * n) == 0 and seq % (2 * n) == 0
    steps = t // (2 * n)
    const = lambda a: pl.BlockSpec(a.shape, lambda j, nd=a.ndim: (0,) * nd, pipeline_mode=pl.Buffered(1))
    small = (h0, tail0, st0) + tuple(mix_prm) + (hgn,)
    in_specs = ([pl.BlockSpec((2 * n, d), lambda j: (jnp.minimum(j, steps - 1), 0)),
                 pl.BlockSpec((2 * n, d), lambda j: (jnp.maximum(j - 1, 0), 0)),
                 const(mixg), const(w_in)] + [const(a) for a in small]
                + [const(wo), const(fg), const(wgu), const(wd), const(ng)])
    return pl.pallas_call(
        functools.partial(_layer_kernel, seq // n),
        grid=(steps + 1,),
        in_specs=in_specs,
        out_specs=pl.BlockSpec((2 * n, d), lambda j: (jnp.maximum(j - 1, 0), 0)),
        out_shape=jax.ShapeDtypeStruct((t, d), _F32),
        scratch_shapes=[
            pltpu.VMEM((n, D_IN), _F32), pltpu.VMEM((n, D_IN), _F32),
            pltpu.VMEM((n, D_MODEL), _BF16), pltpu.VMEM((n, D_MODEL), _BF16),
            pltpu.VMEM((n, D_FF), _BF16),
            pltpu.VMEM((n, D_MODEL), _F32),
            pltpu.VMEM((n, 2 * D_RG), _F32),
            pltpu.VMEM((n, D_RG), _F32),
            pltpu.VMEM((n + SUBLANES, D_RG), _F32),
            pltpu.VMEM((SUBLANES, D_RG), _F32),
            pltpu.VMEM((HG_HEADS, HG_HEAD_DIM, HG_HEAD_DIM), _F32),
        ],
        compiler_params=pltpu.CompilerParams(
            dimension_semantics=("arbitrary",), vmem_limit_bytes=VMEM_LIMIT),
        name="layer",
    )(x2d, x2d, mixg, w_in, *small, wo, fg, wgu, wd, ng)


def _pair_gate_weights(w_r, w_i):
    def pairs(w):
        h, d, _ = w.shape
        w = w.reshape(h // 2, 2, d, d)
        eye = jnp.eye(2, dtype=w.dtype)
        return (eye[None, :, None, :, None] * w[:, :, :, None, :]).reshape(h // 2, 2 * d, 2 * d)
    return jnp.concatenate([pairs(w_r), pairs(w_i)], axis=2)


def kernel(x, meta_tokens, mix_norm_g, w_in, conv_w, conv_b, w_rgate, b_rgate, w_igate, b_igate,
           lru_lambda, rg_norm_g, hg_lower_bound, hg_norm_g, w_out, ffn_norm_g, w_gate_up, w_down,
           final_norm_g):
    b, s, d = x.shape
    row = lambda a: a.reshape(1, -1).astype(_F32)
    skew = lambda wt: jnp.pad(wt, ((0, 0), (0, LANES))).astype(_BF16)
    w_in_b = skew(w_in[0])
    wg = _pair_gate_weights(w_rgate[0], w_igate[0]).astype(_BF16)
    mixg = row(mix_norm_g[0])
    mix_prm = (conv_w[0].astype(_F32), row(conv_b[0]), wg, row(b_rgate[0]), row(b_igate[0]),
               row(lru_lambda[0]), row(rg_norm_g[0]), hg_lower_bound.astype(_F32))

    p_meta = _norm_proj(meta_tokens.astype(x.dtype), mixg, w_in_b)
    h0, tail0, st0 = _meta_state(p_meta, *mix_prm)
    out = _layer(x.reshape(b * s, d), s, mixg, w_in_b, h0, tail0, st0, mix_prm, row(hg_norm_g[0]),
                 skew(w_out[0]), row(ffn_norm_g[0]), w_gate_up[0].astype(_BF16),
                 skew(w_down[0]), row(final_norm_g))
    return out.reshape(b, s, d)
```

```python
import functools

import jax
import jax.numpy as jnp
from jax import lax
from jax.experimental import pallas as pl
from jax.experimental.pallas import tpu as pltpu

D_MODEL = 1024
N_META = 16
D_RG = 512
CONV_W = 4
LRU_C = 8.0
D_HG = 512
HG_HEAD_DIM = 128
HG_HEADS = 4
D_IN = 2 * D_RG + 4 * D_HG
D_FF = 2816
EPS = 1e-6

GATE_PAIR = 128
HG_CHUNK = 64
SUBLANES = 8
LANES = 128
BLOCK_ROWS = 256
IN_TILE = 512
FF_TILE = 256
VMEM_LIMIT = 60 * 1024 * 1024

_F32 = jnp.float32
_BF16 = jnp.bfloat16
_NT = (((1,), (1,)), ((), ()))
_TN = (((0,), (0,)), ((), ()))


def _rms(x, g):
    return x * lax.rsqrt(jnp.mean(x * x, axis=-1, keepdims=True) + EPS) * g


def _silu(x):
    hx = 0.5 * x
    return hx + hx * jnp.tanh(hx)


def _scaled_sigmoid(x, scale):
    hs = 0.5 * scale
    return hs + hs * jnp.tanh(0.5 * x)


def _gelu_tanh(x):
    c = 0.7978845608028654
    t = jnp.tanh(x * (c + (c * 0.044715) * (x * x)))
    return (0.5 * x) * (1.0 + t)


def _dot(a, b):
    return lax.dot_general(a, b, (((1,), (0,)), ((), ())), preferred_element_type=_F32)


def _slab_roll(x, shift):
    groups = [pltpu.roll(x[i:i + SUBLANES], shift, 0) for i in range(0, x.shape[0], SUBLANES)]
    return jnp.concatenate(groups, axis=0)


def _rg_gates(xr, xbuf, prm):
    conv_w, conv_b, w_gate = prm[0], prm[1], prm[2]
    n = xr.shape[0]
    xbuf[SUBLANES:SUBLANES + n, :] = xr
    xc = conv_b + conv_w[3:4, :] * xr
    for j in range(1, CONV_W):
        xc = xc + conv_w[CONV_W - 1 - j:CONV_W - j, :] * xbuf[SUBLANES - j:SUBLANES - j + n, :]
    xbuf[0:SUBLANES, :] = xr[n - SUBLANES:n, :]
    xb = xc.astype(_BF16)
    parts = [_dot(xb[:, p * GATE_PAIR:(p + 1) * GATE_PAIR], w_gate[p]) for p in range(D_RG // GATE_PAIR)]
    gates = jnp.concatenate([part[:, :GATE_PAIR] for part in parts] + [part[:, GATE_PAIR:] for part in parts],
                            axis=1)
    return xc, gates


def _rg_rows(xc, gates, gr, h_ref, prm):
    b_r, b_i, lam, norm_g = prm[3], prm[4], prm[5], prm[6]
    n = xc.shape[0]
    z = -lam
    softplus = jnp.maximum(z, 0.0) + jnp.log1p(jnp.exp(-jnp.abs(z)))
    log_a = _scaled_sigmoid(gates[:, :D_RG] + b_r, -LRU_C * softplus)
    a = jnp.exp(log_a)
    ix = _scaled_sigmoid(gates[:, D_RG:] + b_i, xc)
    bx = jnp.sqrt(-jnp.tanh(log_a) * (a * a + 1.0)) * ix

    pos = lax.broadcasted_iota(jnp.int32, (n, 1), 0) & (SUBLANES - 1)
    for s in (1, 2, 4):
        keep = pos >= s
        a_prev = _slab_roll(a, s)
        b_prev = _slab_roll(bx, s)
        bx = jnp.where(keep, a * b_prev + bx, bx)
        a = jnp.where(keep, a * a_prev, a)
    h = h_ref[...]
    hs = []
    for g in range(n // SUBLANES):
        rows = slice(g * SUBLANES, (g + 1) * SUBLANES)
        hg = a[rows] * h + bx[rows]
        hs.append(hg)
        h = jnp.broadcast_to(hg[SUBLANES - 1:SUBLANES, :], (SUBLANES, D_RG))
    h_ref[...] = h
    y = _gelu_tanh(gr) * jnp.concatenate(hs, axis=0)
    return _rms(y, norm_g)


def _hg_stages(hq, hf, hi, lb, st_ref, head, finish, valid=None):
    c = HG_CHUNK
    q = _silu(hq)
    half = 0.5 * (1.0 - lb)
    f = (lb + half) + half * jnp.tanh(0.5 * hf)
    v = hi
    if valid is not None:
        q = jnp.where(valid, q, 0.0)
        f = jnp.where(valid, f, 1.0)
        v = jnp.where(valid, v, 0.0)
    k = 1.0 - f
    vb = v.astype(_BF16)

    row = lax.broadcasted_iota(jnp.int32, (c, 1), 0)
    pos = row & (SUBLANES - 1)

    kd = k
    vd = v
    o_band = jnp.sum(q * kd, axis=-1, keepdims=True) * vd
    for d in range(1, SUBLANES):
        kd = _slab_roll(kd, 1) * f
        vd = _slab_roll(vd, 1)
        a = jnp.sum(q * kd, axis=-1, keepdims=True)
        o_band = o_band + jnp.where(pos >= d, a, 0.0) * vd

    tot = f
    pre = f
    suf = jnp.ones_like(f)
    m = 1
    while m < SUBLANES:
        upper = (row & m) != 0
        sib = jnp.where(upper, _slab_roll(tot, m), _slab_roll(tot, SUBLANES - m))
        pre = jnp.where(upper, pre * sib, pre)
        suf = jnp.where(upper, suf, suf * sib)
        tot = tot * sib
        m *= 2

    ns = c // SUBLANES
    slabs = lambda a: [a[i * SUBLANES:(i + 1) * SUBLANES] for i in range(ns)]
    q_s, k_s, pre_s, suf_s, tot_s = slabs(q), slabs(k), slabs(pre), slabs(suf), slabs(tot)
    zero = jnp.zeros((SUBLANES, q.shape[1]), _F32)
    operands = []
    g = 1
    while g < ns:
        lhs = jnp.concatenate([q_s[i] * pre_s[i] if i & g else zero for i in range(ns)], axis=0)
        rhs = jnp.concatenate([zero if i & g else k_s[i] * suf_s[i] for i in range(ns)], axis=0)
        operands.append((g * SUBLANES, lhs.astype(_BF16), rhs.astype(_BF16)))
        sib = [tot_s[i ^ g] for i in range(ns)]
        pre_s = [pre_s[i] * sib[i] if i & g else pre_s[i] for i in range(ns)]
        suf_s = [suf_s[i] if i & g else suf_s[i] * sib[i] for i in range(ns)]
        tot_s = [tot_s[i] * sib[i] for i in range(ns)]
        g *= 2
    qp = (q * jnp.concatenate(pre_s, axis=0)).astype(_BF16)
    ks = (k * jnp.concatenate(suf_s, axis=0)).astype(_BF16)
    tot = tot_s[0]
    yield

    st = st_ref[head]
    prods = [(m, lax.dot_general(lhs, rhs, _NT, preferred_element_type=_F32)) for m, lhs, rhs in operands]
    inter = lax.dot_general(qp, st.astype(_BF16), _NT, preferred_element_type=_F32)
    upd = lax.dot_general(vb, ks, _TN, preferred_element_type=_F32)
    yield

    ti = lax.broadcasted_iota(jnp.int32, (c, c), 0)
    si = lax.broadcasted_iota(jnp.int32, (c, c), 1)
    att = None
    for m, prod in prods:
        if 2 * m < c:
            shift = (2 * m).bit_length() - 1
            prod = jnp.where((ti >> shift) == (si >> shift), prod, 0.0)
        att = prod if att is None else att + prod
    intra = _dot(att.astype(_BF16), vb)
    st_ref[head] = st * tot[0:1, :] + upd
    yield

    finish(inter + intra + o_band)


def _hg_chunk(hq, hf, hi, lb, st_ref, head, valid=None):
    out = []
    for _ in _hg_stages(hq, hf, hi, lb, st_ref, head, out.append, valid=valid):
        pass
    return out[0]


def _hg_lower_bound(lb2):
    m = jnp.maximum(lb2[0:1, :], lb2[1:2, :])
    e0 = jnp.exp(lb2[0:1, :] - m)
    e1 = jnp.exp(lb2[1:2, :] - m)
    return e0 / (e0 + e1)


def _norm_proj_kernel(x_ref, g_ref, w_ref, o_ref):
    u = _rms(x_ref[...], g_ref[...]).astype(_BF16)
    o_ref[...] = _dot(u, w_ref[...])


def _norm_proj(x2d, g, w_bf16):
    t, d = x2d.shape
    n = w_bf16.shape[1]
    full = lambda a: pl.BlockSpec(a.shape, lambda: (0,) * a.ndim)
    return pl.pallas_call(
        _norm_proj_kernel,
        in_specs=[full(x2d), full(g), full(w_bf16)],
        out_specs=pl.BlockSpec((t, n), lambda: (0, 0)),
        out_shape=jax.ShapeDtypeStruct((t, n), _F32),
        compiler_params=pltpu.CompilerParams(vmem_limit_bytes=VMEM_LIMIT),
        name="meta_proj",
    )(x2d, g, w_bf16)


def _meta_state_kernel(p_ref, cw_ref, cb_ref, wg_ref, br_ref, bi_ref, lam_ref, rgn_ref, lb_ref,
                       h_out, tail_out, st_out, xbuf):
    n = N_META
    xbuf[...] = jnp.zeros_like(xbuf)
    h_out[...] = jnp.zeros_like(h_out)
    prm = (cw_ref[...], cb_ref[...], wg_ref[...], br_ref[...], bi_ref[...], lam_ref[...], rgn_ref[...])
    xc, gates = _rg_gates(p_ref[:, 0:D_RG], xbuf, prm)
    _rg_rows(xc, gates, p_ref[:, D_RG:2 * D_RG], h_out, prm)
    tail_out[...] = xbuf[0:SUBLANES, :]

    st_out[...] = jnp.zeros_like(st_out)
    lb = _hg_lower_bound(lb_ref[...])
    pad = HG_CHUNK - n
    valid = lax.broadcasted_iota(jnp.int32, (HG_CHUNK, 1), 0) >= pad
    zeros = jnp.zeros((pad, HG_HEAD_DIM), _F32)
    base = 2 * D_RG
    for hd in range(HG_HEADS):
        lo = hd * HG_HEAD_DIM
        cols = lambda part: slice(base + part * D_HG + lo, base + part * D_HG + lo + HG_HEAD_DIM)
        hq = jnp.concatenate([zeros, p_ref[:, cols(0)]], axis=0)
        hf = jnp.concatenate([zeros, p_ref[:, cols(1)]], axis=0)
        hi = jnp.concatenate([zeros, p_ref[:, cols(2)]], axis=0)
        _hg_chunk(hq, hf, hi, lb[:, lo:lo + HG_HEAD_DIM], st_out, hd, valid=valid)


def _meta_state(p_meta, cw, cb, wg, br, bi, lam, rgn, lb2):
    full = lambda a: pl.BlockSpec(a.shape, lambda: (0,) * a.ndim)
    args = (p_meta, cw, cb, wg, br, bi, lam, rgn, lb2)
    out_shapes = (
        jax.ShapeDtypeStruct((SUBLANES, D_RG), _F32),
        jax.ShapeDtypeStruct((SUBLANES, D_RG), _F32),
        jax.ShapeDtypeStruct((HG_HEADS, HG_HEAD_DIM, HG_HEAD_DIM), _F32),
    )
    return pl.pallas_call(
        _meta_state_kernel,
        in_specs=[full(a) for a in args],
        out_specs=tuple(pl.BlockSpec(s.shape, lambda n=len(s.shape): (0,) * n) for s in out_shapes),
        out_shape=out_shapes,
        scratch_shapes=[pltpu.VMEM((N_META + SUBLANES, D_RG), _F32)],
        name="meta_state",
    )(*args)


def _interleave(streams):
    totals = [sum(c for c, _ in st) for st in streams]
    pos = [0] * len(streams)
    done = [0.0] * len(streams)
    order = []
    while any(p < len(st) for p, st in zip(pos, streams)):
        k = min((i for i in range(len(streams)) if pos[i] < len(streams[i])),
                key=lambda i: done[i] / totals[i])
        cost, thunk = streams[k][pos[k]]
        order.append(thunk)
        pos[k] += 1
        done[k] += cost
    return order


def _tick(xa, xc, o_w, p_w, p_r, y_w, y_r, w, s):
    n = BLOCK_ROWS
    prm, lb, hgn = w["prm"], w["lb"], w["hgn"]
    mxu, vpu = [], []

    val = {}

    def a_norm():
        val["u"] = _rms(xa[...], w["mixg"]).astype(_BF16)
    mxu.append((400, a_norm))
    for t in range(D_IN // IN_TILE):
        def a_tile(t=t):
            cols = slice(t * IN_TILE, (t + 1) * IN_TILE)
            p_w[:, cols] = _dot(val["u"], w["w_in"][:, cols])
        mxu.append((512, a_tile))

    for t in range(D_MODEL // FF_TILE):
        def c_proj(t=t):
            if "y" not in val:
                val["y"] = y_r[...]
            cols = slice(t * FF_TILE, (t + 1) * FF_TILE)
            s["h1"][:, cols] = xc[:, cols] + _dot(val["y"], w["w_out"][:, cols])
        mxu.append((300, c_proj))

    def c_rms():
        val["v"] = _rms(s["h1"][...], w["ffn_g"]).astype(_BF16)
    mxu.append((250, c_rms))
    pending = {}
    n_ff = D_FF // FF_TILE
    for j in range(n_ff + 1):
        def c_gate_up(j=j):
            lo = j * FF_TILE
            pending[j] = (_dot(val["v"], w["w_gu"][:, lo:lo + FF_TILE]),
                          _dot(val["v"], w["w_gu"][:, D_FF + lo:D_FF + lo + FF_TILE]))

        def c_act(j=j - 1):
            gate, up = pending.pop(j)
            s["act"][:, j * FF_TILE:(j + 1) * FF_TILE] = (_silu(gate) * up).astype(_BF16)
        if j < n_ff:
            mxu.append((512, c_gate_up))
        if j > 0:
            mxu.append((100, c_act))
    for t in range(D_MODEL // FF_TILE):
        def c_down(t=t):
            if "act" not in val:
                val["act"] = s["act"][...]
            cols = slice(t * FF_TILE, (t + 1) * FF_TILE)
            o_w[:, cols] = s["h1"][:, cols] + _dot(val["act"], w["w_down"][:, cols])
        mxu.append((750, c_down))

    def c_norm():
        o_w[...] = _rms(o_w[...], w["final_g"])
    mxu.append((400, c_norm))

    def b_gates():
        xconv, gates = _rg_gates(p_r[:, 0:D_RG], s["xbuf"], prm)
        s["xconv"][...] = xconv
        s["gates"][...] = gates
    vpu.append((900, b_gates))
    for g in range(n // HG_CHUNK):
        def b_rg(g=g):
            rows = slice(g * HG_CHUNK, (g + 1) * HG_CHUNK)
            y = _rg_rows(s["xconv"][rows, :], s["gates"][rows, :], p_r[rows, D_RG:2 * D_RG], s["h"], prm)
            y_w[rows, 0:D_RG] = y.astype(_BF16)
        vpu.append((750, b_rg))
    base = 2 * D_RG

    def hg_piece(ci, hd):
        rows = slice(ci * HG_CHUNK, (ci + 1) * HG_CHUNK)
        lo = hd * HG_HEAD_DIM
        col = lambda part: slice(base + part * D_HG + lo, base + part * D_HG + lo + HG_HEAD_DIM)

        def finish(o):
            o = _rms(o, hgn) * _silu(p_r[rows, col(3)])
            y_w[rows, D_RG + lo:D_RG + lo + HG_HEAD_DIM] = o.astype(_BF16)
        return _hg_stages(p_r[rows, col(0)], p_r[rows, col(1)], p_r[rows, col(2)],
                          lb[:, lo:lo + HG_HEAD_DIM], s["st"], hd, finish)

    pieces = [(ci, hd) for ci in range(n // HG_CHUNK) for hd in range(HG_HEADS)]
    gens = {}
    stage_cost = (260, 40, 40, 40)

    def advance(i):
        if i not in gens:
            gens[i] = hg_piece(*pieces[i])
        next(gens[i], None)

    for r in range(len(pieces) + len(stage_cost) - 1):
        for stage in (1, 2, 3, 0):
            i = r - stage
            if 0 <= i < len(pieces):
                vpu.append((stage_cost[stage], lambda i=i: advance(i)))

    for thunk in _interleave([mxu, vpu]):
        thunk()


def _layer_kernel(blocks_per_seq,
                  xa_ref, xc_ref, small_ref, win_ref, h0_ref, tail0_ref, st0_ref, wg_ref,
                  wo_ref, wgu_ref, wd_ref,
                  o_ref, p0, p1, y0, y1, act, h1_s, gates_s, xconv_s, xbuf, h_s, st_s):
    j = pl.program_id(0)
    n = BLOCK_ROWS

    @pl.when(j == 0)
    def _():
        p1[...] = jnp.zeros_like(p1)
        y0[...] = jnp.zeros_like(y0)
        y1[...] = jnp.zeros_like(y1)
        xbuf[0:SUBLANES, :] = jnp.zeros((SUBLANES, D_RG), _F32)
        h_s[...] = jnp.zeros_like(h_s)
        st_s[...] = jnp.zeros_like(st_s)

    sm = lambda name: _small_get(small_ref, name)
    w = dict(prm=(sm("conv_w"), sm("conv_b"), wg_ref[...], sm("b_r"), sm("b_i"), sm("lam"), sm("rg_g")),
             lb=_hg_lower_bound(sm("hg_lb")), hgn=sm("hg_g"), mixg=sm("mix_g"), w_in=win_ref,
             w_out=wo_ref, ffn_g=sm("ffn_g"), w_gu=wgu_ref, w_down=wd_ref, final_g=sm("final_g"))
    s = dict(h1=h1_s, act=act, gates=gates_s, xconv=xconv_s, xbuf=xbuf, h=h_s, st=st_s)
    lo, hi = pl.ds(0, n), pl.ds(n, n)

    _tick(xa_ref.at[lo], xc_ref.at[lo], o_ref.at[lo], p0, p1, y1, y0, w, s)

    @pl.when(lax.rem(2 * j, blocks_per_seq) == 0)
    def _():
        h_s[...] = h0_ref[...]
        xbuf[0:SUBLANES, :] = tail0_ref[...]
        st_s[...] = st0_ref[...]

    _tick(xa_ref.at[hi], xc_ref.at[hi], o_ref.at[hi], p1, p0, y0, y1, w, s)


_SMALL_LAYOUT = {
    "mix_g": (0, 1, D_MODEL), "ffn_g": (1, 1, D_MODEL), "final_g": (2, 1, D_MODEL),
    "conv_w": (3, CONV_W, D_RG), "conv_b": (7, 1, D_RG), "b_r": (8, 1, D_RG), "b_i": (9, 1, D_RG),
    "lam": (10, 1, D_RG), "rg_g": (11, 1, D_RG), "hg_lb": (12, 2, D_HG), "hg_g": (14, 1, HG_HEAD_DIM),
}
SMALL_ROWS = 16


def _small_pack(**arrays):
    out = jnp.zeros((SMALL_ROWS, D_MODEL), _F32)
    for name, (r0, rows, cols) in _SMALL_LAYOUT.items():
        out = out.at[r0:r0 + rows, 0:cols].set(arrays[name].reshape(rows, cols).astype(_F32))
    return out


def _small_get(ref, name):
    r0, rows, cols = _SMALL_LAYOUT[name]
    return ref[r0:r0 + rows, 0:cols]


def _layer(x2d, seq, small, w_in, h0, tail0, st0, wg, wo, wgu, wd):
    t, d = x2d.shape
    n = BLOCK_ROWS
    assert t % (2 * n) == 0 and seq % (2 * n) == 0
    steps = t // (2 * n)
    const = lambda a: pl.BlockSpec(a.shape, lambda j, nd=a.ndim: (0,) * nd, pipeline_mode=pl.Buffered(1))
    consts = (small, w_in, h0, tail0, st0, wg, wo, wgu, wd)
    in_specs = ([pl.BlockSpec((2 * n, d), lambda j: (jnp.minimum(j, steps - 1), 0)),
                 pl.BlockSpec((2 * n, d), lambda j: (jnp.maximum(j - 1, 0), 0))]
                + [const(a) for a in consts])
    return pl.pallas_call(
        functools.partial(_layer_kernel, seq // n),
        grid=(steps + 1,),
        in_specs=in_specs,
        out_specs=pl.BlockSpec((2 * n, d), lambda j: (jnp.maximum(j - 1, 0), 0)),
        out_shape=jax.ShapeDtypeStruct((t, d), _F32),
        scratch_shapes=[
            pltpu.VMEM((n, D_IN), _F32), pltpu.VMEM((n, D_IN), _F32),
            pltpu.VMEM((n, D_MODEL), _BF16), pltpu.VMEM((n, D_MODEL), _BF16),
            pltpu.VMEM((n, D_FF), _BF16),
            pltpu.VMEM((n, D_MODEL), _F32),
            pltpu.VMEM((n, 2 * D_RG), _F32),
            pltpu.VMEM((n, D_RG), _F32),
            pltpu.VMEM((n + SUBLANES, D_RG), _F32),
            pltpu.VMEM((SUBLANES, D_RG), _F32),
            pltpu.VMEM((HG_HEADS, HG_HEAD_DIM, HG_HEAD_DIM), _F32),
        ],
        compiler_params=pltpu.CompilerParams(
            dimension_semantics=("arbitrary",), vmem_limit_bytes=VMEM_LIMIT),
        name="layer",
    )(x2d, x2d, *consts)


def _pair_gate_weights(w_r, w_i):
    def pairs(w):
        h, d, _ = w.shape
        w = w.reshape(h // 2, 2, d, d)
        eye = jnp.eye(2, dtype=w.dtype)
        return (eye[None, :, None, :, None] * w[:, :, :, None, :]).reshape(h // 2, 2 * d, 2 * d)
    return jnp.concatenate([pairs(w_r), pairs(w_i)], axis=2)


def kernel(x, meta_tokens, mix_norm_g, w_in, conv_w, conv_b, w_rgate, b_rgate, w_igate, b_igate,
           lru_lambda, rg_norm_g, hg_lower_bound, hg_norm_g, w_out, ffn_norm_g, w_gate_up, w_down,
           final_norm_g):
    b, s, d = x.shape
    row = lambda a: a.reshape(1, -1).astype(_F32)
    skew = lambda wt: jnp.pad(wt.astype(_BF16), ((0, 0), (0, LANES)))
    w_in_b = skew(w_in[0])
    wg = _pair_gate_weights(w_rgate[0], w_igate[0]).astype(_BF16)
    mixg = row(mix_norm_g[0])
    mix_prm = (conv_w[0].astype(_F32), row(conv_b[0]), wg, row(b_rgate[0]), row(b_igate[0]),
               row(lru_lambda[0]), row(rg_norm_g[0]), hg_lower_bound.astype(_F32))

    small = _small_pack(mix_g=mix_norm_g[0], ffn_g=ffn_norm_g[0], final_g=final_norm_g, conv_w=conv_w[0],
                        conv_b=conv_b[0], b_r=b_rgate[0], b_i=b_igate[0], lam=lru_lambda[0],
                        rg_g=rg_norm_g[0], hg_lb=hg_lower_bound, hg_g=hg_norm_g[0])

    p_meta = _norm_proj(meta_tokens.astype(x.dtype), mixg, w_in_b)
    h0, tail0, st0 = _meta_state(p_meta, *mix_prm)
    out = _layer(x.reshape(b * s, d), s, small, w_in_b, h0, tail0, st0, wg, skew(w_out[0]),
                 w_gate_up[0].astype(_BF16), skew(w_down[0]))
    return out.reshape(b, s, d)
```

```python
import functools

import jax
import jax.numpy as jnp
from jax import lax
from jax.experimental import pallas as pl
from jax.experimental.pallas import tpu as pltpu

D_MODEL = 1024
N_META = 16
D_RG = 512
CONV_W = 4
LRU_C = 8.0
D_HG = 512
HG_HEAD_DIM = 128
HG_HEADS = 4
D_IN = 2 * D_RG + 4 * D_HG
D_FF = 2816
EPS = 1e-6

GATE_PAIR = 128
HG_CHUNK = 64
RG_STRIP = 128
SUBLANES = 8
LANES = 128
BLOCK_ROWS = 256
IN_TILE = 512
FF_TILE = 256
VMEM_LIMIT = 60 * 1024 * 1024

_F32 = jnp.float32
_BF16 = jnp.bfloat16
_NT = (((1,), (1,)), ((), ()))
_TN = (((0,), (0,)), ((), ()))


def _rms(x, g):
    return x * lax.rsqrt(jnp.mean(x * x, axis=-1, keepdims=True) + EPS) * g


def _silu(x):
    hx = 0.5 * x
    return hx + hx * jnp.tanh(hx)


def _scaled_sigmoid(x, scale):
    hs = 0.5 * scale
    return hs + hs * jnp.tanh(0.5 * x)


def _gelu_tanh(x):
    c = 0.7978845608028654
    t = jnp.tanh(x * (c + (c * 0.044715) * (x * x)))
    return (0.5 * x) * (1.0 + t)


def _dot(a, b):
    return lax.dot_general(a, b, (((1,), (0,)), ((), ())), preferred_element_type=_F32)


def _slab_roll(x, shift):
    groups = [pltpu.roll(x[i:i + SUBLANES], shift, 0) for i in range(0, x.shape[0], SUBLANES)]
    return jnp.concatenate(groups, axis=0)


def _rg_gates(xr, xbuf, prm):
    conv_w, conv_b, w_gate = prm[0], prm[1], prm[2]
    n = xr.shape[0]
    xbuf[SUBLANES:SUBLANES + n, :] = xr
    xc = conv_b + conv_w[3:4, :] * xr
    for j in range(1, CONV_W):
        xc = xc + conv_w[CONV_W - 1 - j:CONV_W - j, :] * xbuf[SUBLANES - j:SUBLANES - j + n, :]
    xbuf[0:SUBLANES, :] = xr[n - SUBLANES:n, :]
    xb = xc.astype(_BF16)
    parts = [_dot(xb[:, p * GATE_PAIR:(p + 1) * GATE_PAIR], w_gate[p]) for p in range(D_RG // GATE_PAIR)]
    gates = jnp.concatenate([part[:, :GATE_PAIR] for part in parts] + [part[:, GATE_PAIR:] for part in parts],
                            axis=1)
    return xc, gates


def _rg_rows(xc, gates, gr, h_ref, prm):
    b_r, b_i, lam, norm_g = prm[3], prm[4], prm[5], prm[6]
    n = xc.shape[0]
    z = -lam
    coef = -LRU_C * (jnp.maximum(z, 0.0) + jnp.log1p(jnp.exp(-jnp.abs(z))))
    pos = lax.broadcasted_iota(jnp.int32, (n, 1), 0) & (SUBLANES - 1)
    ys = []
    for lo in range(0, D_RG, RG_STRIP):
        cols = slice(lo, lo + RG_STRIP)
        log_a = _scaled_sigmoid(gates[:, lo:lo + RG_STRIP] + b_r[:, cols], coef[:, cols])
        a = jnp.exp(log_a)
        ix = _scaled_sigmoid(gates[:, D_RG + lo:D_RG + lo + RG_STRIP] + b_i[:, cols], xc[:, cols])
        bx = jnp.sqrt(-jnp.tanh(log_a) * (a * a + 1.0)) * ix

        for s in (1, 2, 4):
            keep = pos >= s
            a_prev = _slab_roll(a, s)
            b_prev = _slab_roll(bx, s)
            bx = jnp.where(keep, a * b_prev + bx, bx)
            a = jnp.where(keep, a * a_prev, a)
        h = h_ref[:, cols]
        hs = []
        for g in range(n // SUBLANES):
            rows = slice(g * SUBLANES, (g + 1) * SUBLANES)
            hg = a[rows] * h + bx[rows]
            hs.append(hg)
            h = jnp.broadcast_to(hg[SUBLANES - 1:SUBLANES, :], (SUBLANES, RG_STRIP))
        h_ref[:, cols] = h
        ys.append(_gelu_tanh(gr[:, cols]) * jnp.concatenate(hs, axis=0))
    return _rms(jnp.concatenate(ys, axis=1), norm_g)


def _hg_stages(hq, hf, hi, lb, st_ref, head, finish, valid=None):
    c = HG_CHUNK
    q = _silu(hq)
    half = 0.5 * (1.0 - lb)
    f = (lb + half) + half * jnp.tanh(0.5 * hf)
    v = hi
    if valid is not None:
        q = jnp.where(valid, q, 0.0)
        f = jnp.where(valid, f, 1.0)
        v = jnp.where(valid, v, 0.0)
    k = 1.0 - f
    vb = v.astype(_BF16)

    row = lax.broadcasted_iota(jnp.int32, (c, 1), 0)
    pos = row & (SUBLANES - 1)

    kd = k
    vd = v
    o_band = jnp.sum(q * kd, axis=-1, keepdims=True) * vd
    for d in range(1, SUBLANES):
        kd = _slab_roll(kd, 1) * f
        vd = _slab_roll(vd, 1)
        a = jnp.sum(q * kd, axis=-1, keepdims=True)
        o_band = o_band + jnp.where(pos >= d, a, 0.0) * vd

    tot = f
    pre = f
    suf = jnp.ones_like(f)
    m = 1
    while m < SUBLANES:
        upper = (row & m) != 0
        sib = jnp.where(upper, _slab_roll(tot, m), _slab_roll(tot, SUBLANES - m))
        pre = jnp.where(upper, pre * sib, pre)
        suf = jnp.where(upper, suf, suf * sib)
        tot = tot * sib
        m *= 2

    ns = c // SUBLANES
    slabs = lambda a: [a[i * SUBLANES:(i + 1) * SUBLANES] for i in range(ns)]
    q_s, k_s, pre_s, suf_s, tot_s = slabs(q), slabs(k), slabs(pre), slabs(suf), slabs(tot)
    zero = jnp.zeros((SUBLANES, q.shape[1]), _F32)
    operands = []
    g = 1
    while g < ns:
        lhs = jnp.concatenate([q_s[i] * pre_s[i] if i & g else zero for i in range(ns)], axis=0)
        rhs = jnp.concatenate([zero if i & g else k_s[i] * suf_s[i] for i in range(ns)], axis=0)
        operands.append((g * SUBLANES, lhs.astype(_BF16), rhs.astype(_BF16)))
        sib = [tot_s[i ^ g] for i in range(ns)]
        pre_s = [pre_s[i] * sib[i] if i & g else pre_s[i] for i in range(ns)]
        suf_s = [suf_s[i] if i & g else suf_s[i] * sib[i] for i in range(ns)]
        tot_s = [tot_s[i] * sib[i] for i in range(ns)]
        g *= 2
    qp = (q * jnp.concatenate(pre_s, axis=0)).astype(_BF16)
    ks = (k * jnp.concatenate(suf_s, axis=0)).astype(_BF16)
    tot = tot_s[0]
    yield

    st = st_ref[head]
    prods = [(m, lax.dot_general(lhs, rhs, _NT, preferred_element_type=_F32)) for m, lhs, rhs in operands]
    inter = lax.dot_general(qp, st.astype(_BF16), _NT, preferred_element_type=_F32)
    upd = lax.dot_general(vb, ks, _TN, preferred_element_type=_F32)
    yield

    ti = lax.broadcasted_iota(jnp.int32, (c, c), 0)
    si = lax.broadcasted_iota(jnp.int32, (c, c), 1)
    att = None
    for m, prod in prods:
        if 2 * m < c:
            shift = (2 * m).bit_length() - 1
            prod = jnp.where((ti >> shift) == (si >> shift), prod, 0.0)
        att = prod if att is None else att + prod
    intra = _dot(att.astype(_BF16), vb)
    st_ref[head] = st * tot[0:1, :] + upd
    yield

    finish(inter + intra + o_band)


def _hg_chunk(hq, hf, hi, lb, st_ref, head, valid=None):
    out = []
    for _ in _hg_stages(hq, hf, hi, lb, st_ref, head, out.append, valid=valid):
        pass
    return out[0]


def _hg_lower_bound(lb2):
    m = jnp.maximum(lb2[0:1, :], lb2[1:2, :])
    e0 = jnp.exp(lb2[0:1, :] - m)
    e1 = jnp.exp(lb2[1:2, :] - m)
    return e0 / (e0 + e1)


def _norm_proj_kernel(x_ref, g_ref, w_ref, o_ref):
    u = _rms(x_ref[...], g_ref[...]).astype(_BF16)
    o_ref[...] = _dot(u, w_ref[...])


def _norm_proj(x2d, g, w_bf16):
    t, d = x2d.shape
    n = w_bf16.shape[1]
    full = lambda a: pl.BlockSpec(a.shape, lambda: (0,) * a.ndim)
    return pl.pallas_call(
        _norm_proj_kernel,
        in_specs=[full(x2d), full(g), full(w_bf16)],
        out_specs=pl.BlockSpec((t, n), lambda: (0, 0)),
        out_shape=jax.ShapeDtypeStruct((t, n), _F32),
        compiler_params=pltpu.CompilerParams(vmem_limit_bytes=VMEM_LIMIT),
        name="meta_proj",
    )(x2d, g, w_bf16)


def _meta_state_kernel(p_ref, cw_ref, cb_ref, wg_ref, br_ref, bi_ref, lam_ref, rgn_ref, lb_ref,
                       h_out, tail_out, st_out, xbuf):
    n = N_META
    xbuf[...] = jnp.zeros_like(xbuf)
    h_out[...] = jnp.zeros_like(h_out)
    prm = (cw_ref[...], cb_ref[...], wg_ref[...], br_ref[...], bi_ref[...], lam_ref[...], rgn_ref[...])
    xc, gates = _rg_gates(p_ref[:, 0:D_RG], xbuf, prm)
    _rg_rows(xc, gates, p_ref[:, D_RG:2 * D_RG], h_out, prm)
    tail_out[...] = xbuf[0:SUBLANES, :]

    st_out[...] = jnp.zeros_like(st_out)
    lb = _hg_lower_bound(lb_ref[...])
    pad = HG_CHUNK - n
    valid = lax.broadcasted_iota(jnp.int32, (HG_CHUNK, 1), 0) >= pad
    zeros = jnp.zeros((pad, HG_HEAD_DIM), _F32)
    base = 2 * D_RG
    for hd in range(HG_HEADS):
        lo = hd * HG_HEAD_DIM
        cols = lambda part: slice(base + part * D_HG + lo, base + part * D_HG + lo + HG_HEAD_DIM)
        hq = jnp.concatenate([zeros, p_ref[:, cols(0)]], axis=0)
        hf = jnp.concatenate([zeros, p_ref[:, cols(1)]], axis=0)
        hi = jnp.concatenate([zeros, p_ref[:, cols(2)]], axis=0)
        _hg_chunk(hq, hf, hi, lb[:, lo:lo + HG_HEAD_DIM], st_out, hd, valid=valid)


def _meta_state(p_meta, cw, cb, wg, br, bi, lam, rgn, lb2):
    full = lambda a: pl.BlockSpec(a.shape, lambda: (0,) * a.ndim)
    args = (p_meta, cw, cb, wg, br, bi, lam, rgn, lb2)
    out_shapes = (
        jax.ShapeDtypeStruct((SUBLANES, D_RG), _F32),
        jax.ShapeDtypeStruct((SUBLANES, D_RG), _F32),
        jax.ShapeDtypeStruct((HG_HEADS, HG_HEAD_DIM, HG_HEAD_DIM), _F32),
    )
    return pl.pallas_call(
        _meta_state_kernel,
        in_specs=[full(a) for a in args],
        out_specs=tuple(pl.BlockSpec(s.shape, lambda n=len(s.shape): (0,) * n) for s in out_shapes),
        out_shape=out_shapes,
        scratch_shapes=[pltpu.VMEM((N_META + SUBLANES, D_RG), _F32)],
        name="meta_state",
    )(*args)


def _interleave(streams):
    totals = [sum(c for c, _ in st) for st in streams]
    pos = [0] * len(streams)
    done = [0.0] * len(streams)
    order = []
    while any(p < len(st) for p, st in zip(pos, streams)):
        k = min((i for i in range(len(streams)) if pos[i] < len(streams[i])),
                key=lambda i: done[i] / totals[i])
        cost, thunk = streams[k][pos[k]]
        order.append(thunk)
        pos[k] += 1
        done[k] += cost
    return order


def _tick(xa, xc, o_w, p_w, p_r, y_w, y_r, w, s):
    n = BLOCK_ROWS
    prm, lb, hgn = w["prm"], w["lb"], w["hgn"]
    mxu, vpu = [], []

    val = {}

    def a_norm():
        val["u"] = _rms(xa[...], w["mixg"]).astype(_BF16)
    mxu.append((400, a_norm))
    for t in range(D_IN // IN_TILE):
        def a_tile(t=t):
            cols = slice(t * IN_TILE, (t + 1) * IN_TILE)
            p_w[:, cols] = _dot(val["u"], w["w_in"][:, cols])
        mxu.append((512, a_tile))

    for t in range(D_MODEL // FF_TILE):
        def c_proj(t=t):
            if "y" not in val:
                val["y"] = y_r[...]
            cols = slice(t * FF_TILE, (t + 1) * FF_TILE)
            s["h1"][:, cols] = xc[:, cols] + _dot(val["y"], w["w_out"][:, cols])
        mxu.append((300, c_proj))

    def c_rms():
        val["v"] = _rms(s["h1"][...], w["ffn_g"]).astype(_BF16)
    mxu.append((250, c_rms))
    pending = {}
    n_ff = D_FF // FF_TILE
    for j in range(n_ff + 1):
        def c_gate_up(j=j):
            lo = j * FF_TILE
            pending[j] = (_dot(val["v"], w["w_gu"][:, lo:lo + FF_TILE]),
                          _dot(val["v"], w["w_gu"][:, D_FF + lo:D_FF + lo + FF_TILE]))

        def c_act(j=j - 1):
            gate, up = pending.pop(j)
            s["act"][:, j * FF_TILE:(j + 1) * FF_TILE] = (_silu(gate) * up).astype(_BF16)
        if j < n_ff:
            mxu.append((512, c_gate_up))
        if j > 0:
            mxu.append((100, c_act))
    for t in range(D_MODEL // FF_TILE):
        def c_down(t=t):
            if "act" not in val:
                val["act"] = s["act"][...]
            cols = slice(t * FF_TILE, (t + 1) * FF_TILE)
            o_w[:, cols] = s["h1"][:, cols] + _dot(val["act"], w["w_down"][:, cols])
        mxu.append((750, c_down))

    def c_norm():
        o_w[...] = _rms(o_w[...], w["final_g"])
    mxu.append((400, c_norm))

    def b_gates():
        xconv, gates = _rg_gates(p_r[:, 0:D_RG], s["xbuf"], prm)
        s["xconv"][...] = xconv
        s["gates"][...] = gates
    vpu.append((900, b_gates))
    for g in range(n // HG_CHUNK):
        def b_rg(g=g):
            rows = slice(g * HG_CHUNK, (g + 1) * HG_CHUNK)
            y = _rg_rows(s["xconv"][rows, :], s["gates"][rows, :], p_r[rows, D_RG:2 * D_RG], s["h"], prm)
            y_w[rows, 0:D_RG] = y.astype(_BF16)
        vpu.append((750, b_rg))
    base = 2 * D_RG

    def hg_piece(ci, hd):
        rows = slice(ci * HG_CHUNK, (ci + 1) * HG_CHUNK)
        lo = hd * HG_HEAD_DIM
        col = lambda part: slice(base + part * D_HG + lo, base + part * D_HG + lo + HG_HEAD_DIM)

        def finish(o):
            o = _rms(o, hgn) * _silu(p_r[rows, col(3)])
            y_w[rows, D_RG + lo:D_RG + lo + HG_HEAD_DIM] = o.astype(_BF16)
        return _hg_stages(p_r[rows, col(0)], p_r[rows, col(1)], p_r[rows, col(2)],
                          lb[:, lo:lo + HG_HEAD_DIM], s["st"], hd, finish)

    pieces = [(ci, hd) for ci in range(n // HG_CHUNK) for hd in range(HG_HEADS)]
    gens = {}
    stage_cost = (260, 40, 40, 40)

    def advance(i):
        if i not in gens:
            gens[i] = hg_piece(*pieces[i])
        next(gens[i], None)

    for r in range(len(pieces) + len(stage_cost) - 1):
        for stage in (1, 2, 3, 0):
            i = r - stage
            if 0 <= i < len(pieces):
                vpu.append((stage_cost[stage], lambda i=i: advance(i)))

    for thunk in _interleave([mxu, vpu]):
        thunk()


def _layer_kernel(blocks_per_seq,
                  xa_ref, xc_ref, mixg_ref, win_ref, h0_ref, tail0_ref, st0_ref,
                  cw_ref, cb_ref, wg_ref, br_ref, bi_ref, lam_ref, rgn_ref, lb_ref, hgn_ref,
                  wo_ref, fg_ref, wgu_ref, wd_ref, ng_ref,
                  o_ref, p0, p1, y0, y1, act, h1_s, gates_s, xconv_s, xbuf, h_s, st_s):
    j = pl.program_id(0)
    n = BLOCK_ROWS

    @pl.when(j == 0)
    def _():
        p1[...] = jnp.zeros_like(p1)
        y0[...] = jnp.zeros_like(y0)
        y1[...] = jnp.zeros_like(y1)
        xbuf[0:SUBLANES, :] = jnp.zeros((SUBLANES, D_RG), _F32)
        h_s[...] = jnp.zeros_like(h_s)
        st_s[...] = jnp.zeros_like(st_s)

    w = dict(prm=(cw_ref[...], cb_ref[...], wg_ref[...], br_ref[...], bi_ref[...], lam_ref[...], rgn_ref[...]),
             lb=_hg_lower_bound(lb_ref[...]), hgn=hgn_ref[...], mixg=mixg_ref[...], w_in=win_ref,
             w_out=wo_ref, ffn_g=fg_ref[...], w_gu=wgu_ref, w_down=wd_ref, final_g=ng_ref[...])
    s = dict(h1=h1_s, act=act, gates=gates_s, xconv=xconv_s, xbuf=xbuf, h=h_s, st=st_s)
    lo, hi = pl.ds(0, n), pl.ds(n, n)

    _tick(xa_ref.at[lo], xc_ref.at[lo], o_ref.at[lo], p0, p1, y1, y0, w, s)

    @pl.when(lax.rem(2 * j, blocks_per_seq) == 0)
    def _():
        h_s[...] = h0_ref[...]
        xbuf[0:SUBLANES, :] = tail0_ref[...]
        st_s[...] = st0_ref[...]

    _tick(xa_ref.at[hi], xc_ref.at[hi], o_ref.at[hi], p1, p0, y0, y1, w, s)


def _layer(x2d, seq, mixg, w_in, h0, tail0, st0, mix_prm, hgn, wo, fg, wgu, wd, ng):
    t, d = x2d.shape
    n = BLOCK_ROWS
    assert t % (2 * n) == 0 and seq % (2 * n) == 0
    steps = t // (2 * n)
    const = lambda a: pl.BlockSpec(a.shape, lambda j, nd=a.ndim: (0,) * nd, pipeline_mode=pl.Buffered(1))
    small = (h0, tail0, st0) + tuple(mix_prm) + (hgn,)
    in_specs = ([pl.BlockSpec((2 * n, d), lambda j: (jnp.minimum(j, steps - 1), 0)),
                 pl.BlockSpec((2 * n, d), lambda j: (jnp.maximum(j - 1, 0), 0)),
                 const(mixg), const(w_in)] + [const(a) for a in small]
                + [const(wo), const(fg), const(wgu), const(wd), const(ng)])
    return pl.pallas_call(
        functools.partial(_layer_kernel, seq // n),
        grid=(steps + 1,),
        in_specs=in_specs,
        out_specs=pl.BlockSpec((2 * n, d), lambda j: (jnp.maximum(j - 1, 0), 0)),
        out_shape=jax.ShapeDtypeStruct((t, d), _F32),
        scratch_shapes=[
            pltpu.VMEM((n, D_IN), _F32), pltpu.VMEM((n, D_IN), _F32),
            pltpu.VMEM((n, D_MODEL), _BF16), pltpu.VMEM((n, D_MODEL), _BF16),
            pltpu.VMEM((n, D_FF), _BF16),
            pltpu.VMEM((n, D_MODEL), _F32),
            pltpu.VMEM((n, 2 * D_RG), _F32),
            pltpu.VMEM((n, D_RG), _F32),
            pltpu.VMEM((n + SUBLANES, D_RG), _F32),
            pltpu.VMEM((SUBLANES, D_RG), _F32),
            pltpu.VMEM((HG_HEADS, HG_HEAD_DIM, HG_HEAD_DIM), _F32),
        ],
        compiler_params=pltpu.CompilerParams(
            dimension_semantics=("arbitrary",), vmem_limit_bytes=VMEM_LIMIT),
        name="layer",
    )(x2d, x2d, mixg, w_in, *small, wo, fg, wgu, wd, ng)


def _pair_gate_weights(w_r, w_i):
    def pairs(w):
        h, d, _ = w.shape
        w = w.reshape(h // 2, 2, d, d)
        eye = jnp.eye(2, dtype=w.dtype)
        return (eye[None, :, None, :, None] * w[:, :, :, None, :]).reshape(h // 2, 2 * d, 2 * d)
    return jnp.concatenate([pairs(w_r), pairs(w_i)], axis=2)


def kernel(x, meta_tokens, mix_norm_g, w_in, conv_w, conv_b, w_rgate, b_rgate, w_igate, b_igate,
           lru_lambda, rg_norm_g, hg_lower_bound, hg_norm_g, w_out, ffn_norm_g, w_gate_up, w_down,
           final_norm_g):
    b, s, d = x.shape
    row = lambda a: a.reshape(1, -1).astype(_F32)
    skew = lambda wt: jnp.pad(wt.astype(_BF16), ((0, 0), (0, LANES)))
    w_in_b = skew(w_in[0])
    wg = _pair_gate_weights(w_rgate[0], w_igate[0]).astype(_BF16)
    mixg = row(mix_norm_g[0])
    mix_prm = (conv_w[0].astype(_F32), row(conv_b[0]), wg, row(b_rgate[0]), row(b_igate[0]),
               row(lru_lambda[0]), row(rg_norm_g[0]), hg_lower_bound.astype(_F32))

    p_meta = _norm_proj(meta_tokens.astype(x.dtype), mixg, w_in_b)
    h0, tail0, st0 = _meta_state(p_meta, *mix_prm)
    out = _layer(x.reshape(b * s, d), s, mixg, w_in_b, h0, tail0, st0, mix_prm, row(hg_norm_g[0]),
                 skew(w_out[0]), row(ffn_norm_g[0]), w_gate_up[0].astype(_BF16),
                 skew(w_down[0]), row(final_norm_g))
    return out.reshape(b, s, d)
```

```python
import functools

import jax
import jax.numpy as jnp
from jax import lax
from jax.experimental import pallas as pl
from jax.experimental.pallas import tpu as pltpu

D_MODEL = 1024
N_META = 16
D_RG = 512
CONV_W = 4
LRU_C = 8.0
D_HG = 512
HG_HEAD_DIM = 128
HG_HEADS = 4
D_IN = 2 * D_RG + 4 * D_HG
D_FF = 2816
EPS = 1e-6

GATE_PAIR = 128
HG_CHUNK = 64
RG_STRIP = 128
SUBLANES = 8
LANES = 128
BLOCK_ROWS = 256
IN_TILE = 512
FF_TILE = 256
VMEM_LIMIT = 60 * 1024 * 1024

_F32 = jnp.float32
_BF16 = jnp.bfloat16
_NT = (((1,), (1,)), ((), ()))
_TN = (((0,), (0,)), ((), ()))


def _rms(x, g):
    return x * lax.rsqrt(jnp.mean(x * x, axis=-1, keepdims=True) + EPS) * g


def _silu(x):
    hx = 0.5 * x
    return hx + hx * jnp.tanh(hx)


def _scaled_sigmoid(x, scale):
    hs = 0.5 * scale
    return hs + hs * jnp.tanh(0.5 * x)


def _gelu_tanh(x):
    c = 0.7978845608028654
    t = jnp.tanh(x * (c + (c * 0.044715) * (x * x)))
    return (0.5 * x) * (1.0 + t)


def _dot(a, b):
    return lax.dot_general(a, b, (((1,), (0,)), ((), ())), preferred_element_type=_F32)


def _slab_roll(x, shift):
    groups = [pltpu.roll(x[i:i + SUBLANES], shift, 0) for i in range(0, x.shape[0], SUBLANES)]
    return jnp.concatenate(groups, axis=0)


def _rg_gates(xr, xbuf, prm):
    conv_w, conv_b, w_gate = prm[0], prm[1], prm[2]
    n = xr.shape[0]
    xbuf[SUBLANES:SUBLANES + n, :] = xr
    xc = conv_b + conv_w[3:4, :] * xr
    for j in range(1, CONV_W):
        xc = xc + conv_w[CONV_W - 1 - j:CONV_W - j, :] * xbuf[SUBLANES - j:SUBLANES - j + n, :]
    xbuf[0:SUBLANES, :] = xr[n - SUBLANES:n, :]
    xb = xc.astype(_BF16)
    parts = [_dot(xb[:, p * GATE_PAIR:(p + 1) * GATE_PAIR], w_gate[p]) for p in range(D_RG // GATE_PAIR)]
    gates = jnp.concatenate([part[:, :GATE_PAIR] for part in parts] + [part[:, GATE_PAIR:] for part in parts],
                            axis=1)
    return xc, gates


def _rg_strip(xc, g_r, g_i, gr, h_ref, lo, prm):
    b_r, b_i, lam = prm[3], prm[4], prm[5]
    n = xc.shape[0]
    cols = slice(lo, lo + RG_STRIP)
    z = -lam[:, cols]
    coef = -LRU_C * (jnp.maximum(z, 0.0) + jnp.log1p(jnp.exp(-jnp.abs(z))))
    log_a = _scaled_sigmoid(g_r + b_r[:, cols], coef)
    a = jnp.exp(log_a)
    ix = _scaled_sigmoid(g_i + b_i[:, cols], xc)
    bx = jnp.sqrt(-jnp.tanh(log_a) * (a * a + 1.0)) * ix

    pos = lax.broadcasted_iota(jnp.int32, (n, 1), 0) & (SUBLANES - 1)
    for s in (1, 2, 4):
        keep = pos >= s
        a_prev = _slab_roll(a, s)
        b_prev = _slab_roll(bx, s)
        bx = jnp.where(keep, a * b_prev + bx, bx)
        a = jnp.where(keep, a * a_prev, a)
    h = h_ref[:, cols]
    hs = []
    for g in range(n // SUBLANES):
        rows = slice(g * SUBLANES, (g + 1) * SUBLANES)
        hg = a[rows] * h + bx[rows]
        hs.append(hg)
        h = jnp.broadcast_to(hg[SUBLANES - 1:SUBLANES, :], (SUBLANES, RG_STRIP))
    h_ref[:, cols] = h
    return _gelu_tanh(gr) * jnp.concatenate(hs, axis=0)


def _rg_rows(xc, gates, gr, h_ref, prm):
    ys = [_rg_strip(xc[:, lo:lo + RG_STRIP], gates[:, lo:lo + RG_STRIP],
                    gates[:, D_RG + lo:D_RG + lo + RG_STRIP], gr[:, lo:lo + RG_STRIP], h_ref, lo, prm)
          for lo in range(0, D_RG, RG_STRIP)]
    return _rms(jnp.concatenate(ys, axis=1), prm[6])


def _hg_stages(hq, hf, hi, lb, st_ref, head, finish, valid=None):
    c = HG_CHUNK
    q = _silu(hq)
    half = 0.5 * (1.0 - lb)
    f = (lb + half) + half * jnp.tanh(0.5 * hf)
    v = hi
    if valid is not None:
        q = jnp.where(valid, q, 0.0)
        f = jnp.where(valid, f, 1.0)
        v = jnp.where(valid, v, 0.0)
    k = 1.0 - f
    vb = v.astype(_BF16)

    row = lax.broadcasted_iota(jnp.int32, (c, 1), 0)
    pos = row & (SUBLANES - 1)

    kd = k
    vd = v
    o_band = jnp.sum(q * kd, axis=-1, keepdims=True) * vd
    for d in range(1, SUBLANES):
        kd = _slab_roll(kd, 1) * f
        vd = _slab_roll(vd, 1)
        a = jnp.sum(q * kd, axis=-1, keepdims=True)
        o_band = o_band + jnp.where(pos >= d, a, 0.0) * vd

    tot = f
    pre = f
    suf = jnp.ones_like(f)
    m = 1
    while m < SUBLANES:
        upper = (row & m) != 0
        sib = jnp.where(upper, _slab_roll(tot, m), _slab_roll(tot, SUBLANES - m))
        pre = jnp.where(upper, pre * sib, pre)
        suf = jnp.where(upper, suf, suf * sib)
        tot = tot * sib
        m *= 2

    ns = c // SUBLANES
    slabs = lambda a: [a[i * SUBLANES:(i + 1) * SUBLANES] for i in range(ns)]
    q_s, k_s, pre_s, suf_s, tot_s = slabs(q), slabs(k), slabs(pre), slabs(suf), slabs(tot)
    zero = jnp.zeros((SUBLANES, q.shape[1]), _F32)
    operands = []
    g = 1
    while g < ns:
        lhs = jnp.concatenate([q_s[i] * pre_s[i] if i & g else zero for i in range(ns)], axis=0)
        rhs = jnp.concatenate([zero if i & g else k_s[i] * suf_s[i] for i in range(ns)], axis=0)
        operands.append((g * SUBLANES, lhs.astype(_BF16), rhs.astype(_BF16)))
        sib = [tot_s[i ^ g] for i in range(ns)]
        pre_s = [pre_s[i] * sib[i] if i & g else pre_s[i] for i in range(ns)]
        suf_s = [suf_s[i] if i & g else suf_s[i] * sib[i] for i in range(ns)]
        tot_s = [tot_s[i] * sib[i] for i in range(ns)]
        g *= 2
    qp = (q * jnp.concatenate(pre_s, axis=0)).astype(_BF16)
    ks = (k * jnp.concatenate(suf_s, axis=0)).astype(_BF16)
    tot = tot_s[0]
    yield

    st = st_ref[head]
    prods = [(m, lax.dot_general(lhs, rhs, _NT, preferred_element_type=_F32)) for m, lhs, rhs in operands]
    inter = lax.dot_general(qp, st.astype(_BF16), _NT, preferred_element_type=_F32)
    upd = lax.dot_general(vb, ks, _TN, preferred_element_type=_F32)
    yield

    ti = lax.broadcasted_iota(jnp.int32, (c, c), 0)
    si = lax.broadcasted_iota(jnp.int32, (c, c), 1)
    att = None
    for m, prod in prods:
        if 2 * m < c:
            shift = (2 * m).bit_length() - 1
            prod = jnp.where((ti >> shift) == (si >> shift), prod, 0.0)
        att = prod if att is None else att + prod
    intra = _dot(att.astype(_BF16), vb)
    st_ref[head] = st * tot[0:1, :] + upd
    yield

    finish(inter + intra + o_band)


def _hg_chunk(hq, hf, hi, lb, st_ref, head, valid=None):
    out = []
    for _ in _hg_stages(hq, hf, hi, lb, st_ref, head, out.append, valid=valid):
        pass
    return out[0]


def _hg_lower_bound(lb2):
    m = jnp.maximum(lb2[0:1, :], lb2[1:2, :])
    e0 = jnp.exp(lb2[0:1, :] - m)
    e1 = jnp.exp(lb2[1:2, :] - m)
    return e0 / (e0 + e1)


def _norm_proj_kernel(x_ref, g_ref, w_ref, o_ref):
    u = _rms(x_ref[...], g_ref[...]).astype(_BF16)
    o_ref[...] = _dot(u, w_ref[...])


def _norm_proj(x2d, g, w_bf16):
    t, d = x2d.shape
    n = w_bf16.shape[1]
    full = lambda a: pl.BlockSpec(a.shape, lambda: (0,) * a.ndim)
    return pl.pallas_call(
        _norm_proj_kernel,
        in_specs=[full(x2d), full(g), full(w_bf16)],
        out_specs=pl.BlockSpec((t, n), lambda: (0, 0)),
        out_shape=jax.ShapeDtypeStruct((t, n), _F32),
        compiler_params=pltpu.CompilerParams(vmem_limit_bytes=VMEM_LIMIT),
        name="meta_proj",
    )(x2d, g, w_bf16)


def _meta_state_kernel(p_ref, cw_ref, cb_ref, wg_ref, br_ref, bi_ref, lam_ref, rgn_ref, lb_ref,
                       h_out, tail_out, st_out, xbuf):
    n = N_META
    xbuf[...] = jnp.zeros_like(xbuf)
    h_out[...] = jnp.zeros_like(h_out)
    prm = (cw_ref[...], cb_ref[...], wg_ref[...], br_ref[...], bi_ref[...], lam_ref[...], rgn_ref[...])
    xc, gates = _rg_gates(p_ref[:, 0:D_RG], xbuf, prm)
    _rg_rows(xc, gates, p_ref[:, D_RG:2 * D_RG], h_out, prm)
    tail_out[...] = xbuf[0:SUBLANES, :]

    st_out[...] = jnp.zeros_like(st_out)
    lb = _hg_lower_bound(lb_ref[...])
    pad = HG_CHUNK - n
    valid = lax.broadcasted_iota(jnp.int32, (HG_CHUNK, 1), 0) >= pad
    zeros = jnp.zeros((pad, HG_HEAD_DIM), _F32)
    base = 2 * D_RG
    for hd in range(HG_HEADS):
        lo = hd * HG_HEAD_DIM
        cols = lambda part: slice(base + part * D_HG + lo, base + part * D_HG + lo + HG_HEAD_DIM)
        hq = jnp.concatenate([zeros, p_ref[:, cols(0)]], axis=0)
        hf = jnp.concatenate([zeros, p_ref[:, cols(1)]], axis=0)
        hi = jnp.concatenate([zeros, p_ref[:, cols(2)]], axis=0)
        _hg_chunk(hq, hf, hi, lb[:, lo:lo + HG_HEAD_DIM], st_out, hd, valid=valid)


def _meta_state(p_meta, cw, cb, wg, br, bi, lam, rgn, lb2):
    full = lambda a: pl.BlockSpec(a.shape, lambda: (0,) * a.ndim)
    args = (p_meta, cw, cb, wg, br, bi, lam, rgn, lb2)
    out_shapes = (
        jax.ShapeDtypeStruct((SUBLANES, D_RG), _F32),
        jax.ShapeDtypeStruct((SUBLANES, D_RG), _F32),
        jax.ShapeDtypeStruct((HG_HEADS, HG_HEAD_DIM, HG_HEAD_DIM), _F32),
    )
    return pl.pallas_call(
        _meta_state_kernel,
        in_specs=[full(a) for a in args],
        out_specs=tuple(pl.BlockSpec(s.shape, lambda n=len(s.shape): (0,) * n) for s in out_shapes),
        out_shape=out_shapes,
        scratch_shapes=[pltpu.VMEM((N_META + SUBLANES, D_RG), _F32)],
        name="meta_state",
    )(*args)


def _interleave(streams):
    totals = [sum(c for c, _ in st) for st in streams]
    pos = [0] * len(streams)
    done = [0.0] * len(streams)
    order = []
    while any(p < len(st) for p, st in zip(pos, streams)):
        k = min((i for i in range(len(streams)) if pos[i] < len(streams[i])),
                key=lambda i: done[i] / totals[i])
        cost, thunk = streams[k][pos[k]]
        order.append(thunk)
        pos[k] += 1
        done[k] += cost
    return order


def _tick(xa, xc, o_w, p_w, p_r, y_w, y_r, w, s):
    n = BLOCK_ROWS
    prm, lb, hgn = w["prm"], w["lb"], w["hgn"]
    mxu, vpu = [], []

    val = {}

    def a_norm():
        val["u"] = _rms(xa[...], w["mixg"]).astype(_BF16)
    mxu.append((400, a_norm))
    for t in range(D_IN // IN_TILE):
        def a_tile(t=t):
            cols = slice(t * IN_TILE, (t + 1) * IN_TILE)
            p_w[:, cols] = _dot(val["u"], w["w_in"][:, cols])
        mxu.append((512, a_tile))

    for t in range(D_MODEL // FF_TILE):
        def c_proj(t=t):
            if "y" not in val:
                val["y"] = y_r[...]
            cols = slice(t * FF_TILE, (t + 1) * FF_TILE)
            s["h1"][:, cols] = xc[:, cols] + _dot(val["y"], w["w_out"][:, cols])
        mxu.append((300, c_proj))

    def c_rms():
        val["v"] = _rms(s["h1"][...], w["ffn_g"]).astype(_BF16)
    mxu.append((250, c_rms))
    pending = {}
    n_ff = D_FF // FF_TILE
    for j in range(n_ff + 1):
        def c_gate_up(j=j):
            lo = j * FF_TILE
            pending[j] = (_dot(val["v"], w["w_gu"][:, lo:lo + FF_TILE]),
                          _dot(val["v"], w["w_gu"][:, D_FF + lo:D_FF + lo + FF_TILE]))

        def c_act(j=j - 1):
            gate, up = pending.pop(j)
            s["act"][:, j * FF_TILE:(j + 1) * FF_TILE] = (_silu(gate) * up).astype(_BF16)
        if j < n_ff:
            mxu.append((512, c_gate_up))
        if j > 0:
            mxu.append((100, c_act))
    for t in range(D_MODEL // FF_TILE):
        def c_down(t=t):
            if "act" not in val:
                val["act"] = s["act"][...]
            cols = slice(t * FF_TILE, (t + 1) * FF_TILE)
            o_w[:, cols] = s["h1"][:, cols] + _dot(val["act"], w["w_down"][:, cols])
        mxu.append((750, c_down))

    def c_norm():
        o_w[...] = _rms(o_w[...], w["final_g"])
    mxu.append((400, c_norm))

    def b_gates():
        xconv, gates = _rg_gates(p_r[:, 0:D_RG], s["xbuf"], prm)
        s["xconv"][...] = xconv
        s["gates"][...] = gates
    vpu.append((900, b_gates))
    for lo in range(0, D_RG, RG_STRIP):
        def b_rg(lo=lo):
            cols = slice(lo, lo + RG_STRIP)
            y = _rg_strip(s["xconv"][:, cols], s["gates"][:, cols], s["gates"][:, D_RG + lo:D_RG + lo + RG_STRIP],
                          p_r[:, D_RG + lo:D_RG + lo + RG_STRIP], s["h"], lo, prm)
            s["xconv"][:, cols] = y
        vpu.append((750, b_rg))

    def b_rg_norm():
        y_w[:, 0:D_RG] = _rms(s["xconv"][...], prm[6]).astype(_BF16)
    vpu.append((150, b_rg_norm))
    base = 2 * D_RG

    def hg_piece(ci, hd):
        rows = slice(ci * HG_CHUNK, (ci + 1) * HG_CHUNK)
        lo = hd * HG_HEAD_DIM
        col = lambda part: slice(base + part * D_HG + lo, base + part * D_HG + lo + HG_HEAD_DIM)

        def finish(o):
            o = _rms(o, hgn) * _silu(p_r[rows, col(3)])
            y_w[rows, D_RG + lo:D_RG + lo + HG_HEAD_DIM] = o.astype(_BF16)
        return _hg_stages(p_r[rows, col(0)], p_r[rows, col(1)], p_r[rows, col(2)],
                          lb[:, lo:lo + HG_HEAD_DIM], s["st"], hd, finish)

    pieces = [(ci, hd) for ci in range(n // HG_CHUNK) for hd in range(HG_HEADS)]
    gens = {}
    stage_cost = (260, 40, 40, 40)

    def advance(i):
        if i not in gens:
            gens[i] = hg_piece(*pieces[i])
        next(gens[i], None)

    for r in range(len(pieces) + len(stage_cost) - 1):
        for stage in (1, 2, 3, 0):
            i = r - stage
            if 0 <= i < len(pieces):
                vpu.append((stage_cost[stage], lambda i=i: advance(i)))

    for thunk in _interleave([mxu, vpu]):
        thunk()


def _layer_kernel(blocks_per_seq,
                  xa_ref, xc_ref, mixg_ref, win_ref, h0_ref, tail0_ref, st0_ref,
                  cw_ref, cb_ref, wg_ref, br_ref, bi_ref, lam_ref, rgn_ref, lb_ref, hgn_ref,
                  wo_ref, fg_ref, wgu_ref, wd_ref, ng_ref,
                  o_ref, p0, p1, y0, y1, act, h1_s, gates_s, xconv_s, xbuf, h_s, st_s):
    j = pl.program_id(0)
    n = BLOCK_ROWS

    @pl.when(j == 0)
    def _():
        p1[...] = jnp.zeros_like(p1)
        y0[...] = jnp.zeros_like(y0)
        y1[...] = jnp.zeros_like(y1)
        xbuf[0:SUBLANES, :] = jnp.zeros((SUBLANES, D_RG), _F32)
        h_s[...] = jnp.zeros_like(h_s)
        st_s[...] = jnp.zeros_like(st_s)

    w = dict(prm=(cw_ref[...], cb_ref[...], wg_ref[...], br_ref[...], bi_ref[...], lam_ref[...], rgn_ref[...]),
             lb=_hg_lower_bound(lb_ref[...]), hgn=hgn_ref[...], mixg=mixg_ref[...], w_in=win_ref,
             w_out=wo_ref, ffn_g=fg_ref[...], w_gu=wgu_ref, w_down=wd_ref, final_g=ng_ref[...])
    s = dict(h1=h1_s, act=act, gates=gates_s, xconv=xconv_s, xbuf=xbuf, h=h_s, st=st_s)
    lo, hi = pl.ds(0, n), pl.ds(n, n)

    _tick(xa_ref.at[lo], xc_ref.at[lo], o_ref.at[lo], p0, p1, y1, y0, w, s)

    @pl.when(lax.rem(2 * j, blocks_per_seq) == 0)
    def _():
        h_s[...] = h0_ref[...]
        xbuf[0:SUBLANES, :] = tail0_ref[...]
        st_s[...] = st0_ref[...]

    _tick(xa_ref.at[hi], xc_ref.at[hi], o_ref.at[hi], p1, p0, y0, y1, w, s)


def _layer(x2d, seq, mixg, w_in, h0, tail0, st0, mix_prm, hgn, wo, fg, wgu, wd, ng):
    t, d = x2d.shape
    n = BLOCK_ROWS
    assert t % (2 * n) == 0 and seq % (2 * n) == 0
    steps = t // (2 * n)
    const = lambda a: pl.BlockSpec(a.shape, lambda j, nd=a.ndim: (0,) * nd, pipeline_mode=pl.Buffered(1))
    small = (h0, tail0, st0) + tuple(mix_prm) + (hgn,)
    in_specs = ([pl.BlockSpec((2 * n, d), lambda j: (jnp.minimum(j, steps - 1), 0)),
                 pl.BlockSpec((2 * n, d), lambda j: (jnp.maximum(j - 1, 0), 0)),
                 const(mixg), const(w_in)] + [const(a) for a in small]
                + [const(wo), const(fg), const(wgu), const(wd), const(ng)])
    return pl.pallas_call(
        functools.partial(_layer_kernel, seq // n),
        grid=(steps + 1,),
        in_specs=in_specs,
        out_specs=pl.BlockSpec((2 * n, d), lambda j: (jnp.maximum(j - 1, 0), 0)),
        out_shape=jax.ShapeDtypeStruct((t, d), _F32),
        scratch_shapes=[
            pltpu.VMEM((n, D_IN), _F32), pltpu.VMEM((n, D_IN), _F32),
            pltpu.VMEM((n, D_MODEL), _BF16), pltpu.VMEM((n, D_MODEL), _BF16),
            pltpu.VMEM((n, D_FF), _BF16),
            pltpu.VMEM((n, D_MODEL), _F32),
            pltpu.VMEM((n, 2 * D_RG), _F32),
            pltpu.VMEM((n, D_RG), _F32),
            pltpu.VMEM((n + SUBLANES, D_RG), _F32),
            pltpu.VMEM((SUBLANES, D_RG), _F32),
            pltpu.VMEM((HG_HEADS, HG_HEAD_DIM, HG_HEAD_DIM), _F32),
        ],
        compiler_params=pltpu.CompilerParams(
            dimension_semantics=("arbitrary",), vmem_limit_bytes=VMEM_LIMIT),
        name="layer",
    )(x2d, x2d, mixg, w_in, *small, wo, fg, wgu, wd, ng)


def _pair_gate_weights(w_r, w_i):
    def pairs(w):
        h, d, _ = w.shape
        w = w.reshape(h // 2, 2, d, d)
        eye = jnp.eye(2, dtype=w.dtype)
        return (eye[None, :, None, :, None] * w[:, :, :, None, :]).reshape(h // 2, 2 * d, 2 * d)
    return jnp.concatenate([pairs(w_r), pairs(w_i)], axis=2)


def kernel(x, meta_tokens, mix_norm_g, w_in, conv_w, conv_b, w_rgate, b_rgate, w_igate, b_igate,
           lru_lambda, rg_norm_g, hg_lower_bound, hg_norm_g, w_out, ffn_norm_g, w_gate_up, w_down,
           final_norm_g):
    b, s, d = x.shape
    row = lambda a: a.reshape(1, -1).astype(_F32)
    skew = lambda wt: jnp.pad(wt.astype(_BF16), ((0, 0), (0, LANES)))
    w_in_b = skew(w_in[0])
    wg = _pair_gate_weights(w_rgate[0], w_igate[0]).astype(_BF16)
    mixg = row(mix_norm_g[0])
    mix_prm = (conv_w[0].astype(_F32), row(conv_b[0]), wg, row(b_rgate[0]), row(b_igate[0]),
               row(lru_lambda[0]), row(rg_norm_g[0]), hg_lower_bound.astype(_F32))

    p_meta = _norm_proj(meta_tokens.astype(x.dtype), mixg, w_in_b)
    h0, tail0, st0 = _meta_state(p_meta, *mix_prm)
    out = _layer(x.reshape(b * s, d), s, mixg, w_in_b, h0, tail0, st0, mix_prm, row(hg_norm_g[0]),
                 skew(w_out[0]), row(ffn_norm_g[0]), w_gate_up[0].astype(_BF16),
                 skew(w_down[0]), row(final_norm_g))
    return out.reshape(b, s, d)
```
